```python
import jax, jax.numpy as jnp
from jax import lax
import numpy as np

D_MODEL = 2048
BATCH = 1
SEQ = 8192
DEPTH = 4

N_MIXERS = 2
HEAD_DIM = 128
N_QK_HEADS = D_MODEL // HEAD_DIM
N_V_HEADS = 2 * N_QK_HEADS
QK_DIM = N_QK_HEADS * HEAD_DIM
V_DIM = N_V_HEADS * HEAD_DIM
GDN_CONV_DIM = 2 * QK_DIM + V_DIM
GDN_PROJ_DIM = GDN_CONV_DIM + V_DIM + 4 * N_V_HEADS
SHORT_CONV_WIDTH = 5
CHUNK = 64
CONF_WIDTH = 31
FFN_HIDDEN = (8 * D_MODEL + 3 * 256 - 1) // (3 * 256) * 256
N_GDN_LAYERS = (DEPTH + 1) // 2
N_CONF_LAYERS = DEPTH // 2
RMS_EPS = 1e-6
LN_EPS = 1e-5
L2_EPS = 1e-6

kernel_name = "hybrid_gdn_conformer_encoder"


def rmsnorm(x, w):
    xf = x.astype(jnp.float32)
    y = xf * lax.rsqrt(jnp.mean(xf * xf, axis=-1, keepdims=True) + RMS_EPS)
    return (y * w.astype(jnp.float32)).astype(x.dtype)


def layernorm(x, w, b):
    xf = x.astype(jnp.float32)
    mu = jnp.mean(xf, axis=-1, keepdims=True)
    xc = xf - mu
    y = xc * lax.rsqrt(jnp.mean(xc * xc, axis=-1, keepdims=True) + LN_EPS)
    return (y * w.astype(jnp.float32) + b.astype(jnp.float32)).astype(x.dtype)


def l2norm(x):
    return x * lax.rsqrt(jnp.sum(x * x, axis=-1, keepdims=True) + L2_EPS)


def depthwise_conv(x, w):
    K = w.shape[0]
    return lax.conv_general_dilated(
        x, w[:, None, :].astype(x.dtype), window_strides=(1,),
        padding=[(K // 2, K - 1 - K // 2)],
        dimension_numbers=('NWC', 'WIO', 'NWC'),
        feature_group_count=x.shape[-1])


def chunk_gated_delta_rule(q, k, v, g, beta):
    B, T, H, DK = q.shape
    DV = v.shape[-1]
    N = T // CHUNK

    def chunks(t):
        t = t.reshape((B, N, CHUNK, H) + t.shape[3:])
        return t.transpose((1, 0, 3, 2) + tuple(range(4, t.ndim)))

    q = chunks(q) * (DK ** -0.5)
    k = chunks(k)
    v = chunks(v)
    beta = chunks(beta)
    g = jnp.cumsum(chunks(g), axis=-1)
    idx = jnp.arange(CHUNK)
    incl = idx[:, None] >= idx[None, :]
    strict = idx[:, None] > idx[None, :]
    decay = jnp.exp(jnp.where(incl, g[..., :, None] - g[..., None, :], -jnp.inf))
    k_beta = k * beta[..., None]
    lower = jnp.where(strict, jnp.einsum('nbhid,nbhjd->nbhij', k_beta, k) * decay, 0.0)
    eye = jnp.eye(CHUNK, dtype=lower.dtype)
    rhs = jnp.concatenate([v * beta[..., None], k_beta * jnp.exp(g)[..., None]], axis=-1)
    sol = lax.linalg.triangular_solve(eye + lower, rhs, left_side=True, lower=True)
    u, w = sol[..., :DV], sol[..., DV:]
    a_qk = jnp.einsum('nbhid,nbhjd->nbhij', q, k) * decay
    q_g = q * jnp.exp(g)[..., None]
    g_last = g[..., -1:]
    k_tail = k * jnp.exp(g_last - g)[..., None]
    chunk_decay = jnp.exp(g_last)[..., None]

    def step(S, xs):
        u_i, w_i, a_i, qg_i, kt_i, cd_i = xs
        v_new = u_i - jnp.einsum('bhcd,bhde->bhce', w_i, S)
        o_i = jnp.einsum('bhcd,bhde->bhce', qg_i, S) + jnp.einsum('bhij,bhje->bhie', a_i, v_new)
        S = S * cd_i + jnp.einsum('bhcd,bhce->bhde', kt_i, v_new)
        return S, o_i

    S0 = jnp.zeros((B, H, DK, DV), jnp.float32)
    _, o = lax.scan(step, S0, (u, w, a_qk, q_g, k_tail, chunk_decay))
    return o.transpose(1, 0, 3, 2, 4).reshape(B, T, H, DV)


def gated_deltanet(h, in_w, conv_w, A_log, dt_bias, norm_w, out_w):
    B, T, _ = h.shape
    proj = h @ in_w
    qkv = proj[..., :GDN_CONV_DIM]
    z = proj[..., GDN_CONV_DIM:GDN_CONV_DIM + V_DIM]
    ba = proj[..., GDN_CONV_DIM + V_DIM:].astype(jnp.float32).reshape(B, T, 2, 2, N_V_HEADS)
    qkv = jax.nn.silu(depthwise_conv(qkv, conv_w)).astype(jnp.float32)
    q = qkv[..., :QK_DIM].reshape(B, T, N_QK_HEADS, HEAD_DIM)
    k = qkv[..., QK_DIM:2 * QK_DIM].reshape(B, T, N_QK_HEADS, HEAD_DIM)
    v = qkv[..., 2 * QK_DIM:].reshape(B, T, N_V_HEADS, HEAD_DIM)
    rep = N_V_HEADS // N_QK_HEADS
    q = jnp.repeat(l2norm(q), rep, axis=2)
    k = jnp.repeat(l2norm(k), rep, axis=2)
    beta = jax.nn.sigmoid(ba[:, :, 0])
    g = -jnp.exp(A_log.astype(jnp.float32)) * jax.nn.softplus(ba[:, :, 1] + dt_bias.astype(jnp.float32))
    o_f = chunk_gated_delta_rule(q, k, v, g[:, :, 0], beta[:, :, 0])
    flip = lambda t: jnp.flip(t, axis=1)
    o_b = flip(chunk_gated_delta_rule(flip(q), flip(k), flip(v), flip(g[:, :, 1]), flip(beta[:, :, 1])))
    o = o_f + o_b
    o = rmsnorm(o, norm_w) * jax.nn.silu(z.astype(jnp.float32).reshape(B, T, N_V_HEADS, HEAD_DIM))
    return o.reshape(B, T, V_DIM).astype(h.dtype) @ out_w


def conformer_conv(h, pw1_w, pw1_b, dw_w, dw_b, ln_w, ln_b, pw2_w, pw2_b):
    u = h @ pw1_w + pw1_b
    a, gate = jnp.split(u, 2, axis=-1)
    u = a * jax.nn.sigmoid(gate)
    u = depthwise_conv(u, dw_w) + dw_b
    u = jax.nn.silu(layernorm(u, ln_w, ln_b))
    return u @ pw2_w + pw2_b


def swiglu(h, in_w, out_w):
    gu = h @ in_w
    gate, up = jnp.split(gu, 2, axis=-1)
    return (jax.nn.silu(gate) * up) @ out_w


def setup_inputs(seed: int = 0) -> dict:
    key = jax.random.key(seed)
    ks = iter(jax.random.split(key, 32))
    f32 = jnp.float32
    nrm = lambda shape, s: jax.random.normal(next(ks), shape, f32) * s
    D = D_MODEL
    dt = jnp.exp(jax.random.uniform(next(ks), (N_GDN_LAYERS, 2, N_V_HEADS), f32,
                                    float(np.log(1e-3)), float(np.log(1e-1))))
    return {
        "x": nrm((BATCH, SEQ, D), 1.0),
        "c": nrm((BATCH, D), 1.0),
        "ada_w": nrm((DEPTH, D, 6 * D), 0.5 * D ** -0.5),
        "ada_b": nrm((DEPTH, 6 * D), 0.02),
        "norm_w": 1.0 + nrm((DEPTH, 4, D), 0.1),
        "gdn_in_w": nrm((N_GDN_LAYERS, D, GDN_PROJ_DIM), D ** -0.5),
        "gdn_conv_w": nrm((N_GDN_LAYERS, SHORT_CONV_WIDTH, GDN_CONV_DIM), SHORT_CONV_WIDTH ** -0.5),
        "gdn_A_log": jnp.log(jax.random.uniform(next(ks), (N_GDN_LAYERS, 2, N_V_HEADS), f32, 1.0, 16.0)),
        "gdn_dt_bias": dt + jnp.log(-jnp.expm1(-dt)),
        "gdn_norm_w": 1.0 + nrm((N_GDN_LAYERS, HEAD_DIM), 0.1),
        "gdn_out_w": nrm((N_GDN_LAYERS, V_DIM, D), V_DIM ** -0.5),
        "cf_pw1_w": nrm((N_CONF_LAYERS, D, 2 * D), D ** -0.5),
        "cf_pw1_b": nrm((N_CONF_LAYERS, 2 * D), 0.02),
        "cf_dw_w": nrm((N_CONF_LAYERS, CONF_WIDTH, D), CONF_WIDTH ** -0.5),
        "cf_dw_b": nrm((N_CONF_LAYERS, D), 0.02),
        "cf_ln_w": 1.0 + nrm((N_CONF_LAYERS, D), 0.1),
        "cf_ln_b": nrm((N_CONF_LAYERS, D), 0.02),
        "cf_pw2_w": nrm((N_CONF_LAYERS, D, D), D ** -0.5),
        "cf_pw2_b": nrm((N_CONF_LAYERS, D), 0.02),
        "ffn_in_w": nrm((DEPTH, D, 2 * FFN_HIDDEN), D ** -0.5),
        "ffn_out_w": nrm((DEPTH, FFN_HIDDEN, D), FFN_HIDDEN ** -0.5),
    }


def reference(x, c, ada_w, ada_b, norm_w, gdn_in_w, gdn_conv_w, gdn_A_log, gdn_dt_bias,
              gdn_norm_w, gdn_out_w, cf_pw1_w, cf_pw1_b, cf_dw_w, cf_dw_b, cf_ln_w, cf_ln_b,
              cf_pw2_w, cf_pw2_b, ffn_in_w, ffn_out_w):
    cond = jax.nn.silu(c)
    for i in range(DEPTH):
        mod = (cond @ ada_w[i] + ada_b[i])[:, None, :]
        sh1, sc1, g1, sh2, sc2, g2 = jnp.split(mod, 6, axis=-1)
        h = rmsnorm(x, norm_w[i, 0]) * (1 + sc1) + sh1
        j = i // N_MIXERS
        if i % N_MIXERS == 0:
            h = gated_deltanet(h, gdn_in_w[j], gdn_conv_w[j], gdn_A_log[j], gdn_dt_bias[j],
                               gdn_norm_w[j], gdn_out_w[j])
        else:
            h = conformer_conv(h, cf_pw1_w[j], cf_pw1_b[j], cf_dw_w[j], cf_dw_b[j],
                               cf_ln_w[j], cf_ln_b[j], cf_pw2_w[j], cf_pw2_b[j])
        x = x + (1 + g1) * rmsnorm(h, norm_w[i, 1])
        h = rmsnorm(x, norm_w[i, 2]) * (1 + sc2) + sh2
        h = swiglu(h, ffn_in_w[i], ffn_out_w[i])
        x = x + (1 + g2) * rmsnorm(h, norm_w[i, 3])
    return x
```

```python
import functools

import jax
import jax.numpy as jnp
from jax import lax
from jax.experimental import pallas as pl
from jax.experimental.pallas import tpu as pltpu

F32 = jnp.float32
BF16 = jnp.bfloat16

HEAD_DIM = 128
N_MIXERS = 2
CHUNK = 64
RMS_EPS = 1e-6
LN_EPS = 1e-5
L2_EPS = 1e-6

V7X_VMEM_BYTES = 64 * 1024 * 1024
VMEM_LIMIT = V7X_VMEM_BYTES - 12 * 1024 * 1024
HALO = 16


def _cparams(sem):
    return pltpu.CompilerParams(dimension_semantics=sem, vmem_limit_bytes=VMEM_LIMIT)


def _pick(n, prefs):
    for p in prefs:
        if n % p == 0:
            return p
    return n


def _sigmoid(x):
    return 1.0 / (1.0 + jnp.exp(-x))


def _silu(x):
    return x * _sigmoid(x)


def _rms(y):
    return y * lax.rsqrt(jnp.mean(y * y, axis=-1, keepdims=True) + RMS_EPS)


def _ada_kernel(c_ref, w_ref, b_ref, o_ref):
    cond = _silu(c_ref[...])
    o_ref[...] = jnp.dot(cond.astype(BF16), w_ref[...].astype(BF16),
                         preferred_element_type=F32) + b_ref[...]


def ada_modulation(c, ada_w, ada_b):
    depth, d, n = ada_w.shape
    b = c.shape[0]
    assert b <= 8
    cp = jnp.zeros((8, d), F32).at[:b].set(c)
    tn = _pick(n, (1024, 512, 256, 128))
    out = pl.pallas_call(
        _ada_kernel,
        grid=(depth, n // tn),
        in_specs=[pl.BlockSpec((8, d), lambda l, j: (0, 0)),
                  pl.BlockSpec((None, d, tn), lambda l, j: (l, 0, j)),
                  pl.BlockSpec((None, 1, tn), lambda l, j: (l, 0, j))],
        out_specs=pl.BlockSpec((None, 8, tn), lambda l, j: (l, 0, j)),
        out_shape=jax.ShapeDtypeStruct((depth, 8, n), F32),
        compiler_params=_cparams(("arbitrary", "arbitrary")),
        name="ada_modulation",
    )(cp, ada_w, ada_b.reshape(depth, 1, n))
    return out[:, :b]


def _prenorm_kernel(x_ref, nw_ref, sc_ref, sh_ref, h_ref):
    y = _rms(x_ref[...])
    h_ref[...] = (y * nw_ref[...] * (1.0 + sc_ref[...]) + sh_ref[...]).astype(h_ref.dtype)


def prenorm(x, nw, sc, sh):
    t, d = x.shape
    tm = _pick(t, (512, 256, 128, 64, 8))
    row = pl.BlockSpec((1, d), lambda i: (0, 0))
    return pl.pallas_call(
        _prenorm_kernel,
        grid=(t // tm,),
        in_specs=[pl.BlockSpec((tm, d), lambda i: (i, 0)), row, row, row],
        out_specs=pl.BlockSpec((tm, d), lambda i: (i, 0)),
        out_shape=jax.ShapeDtypeStruct((t, d), BF16),
        compiler_params=_cparams(("arbitrary",)),
        name="prenorm",
    )(x, nw, sc, sh)


def _mm_kernel(x_ref, w_ref, o_ref):
    o_ref[...] = jnp.dot(x_ref[...], w_ref[...], preferred_element_type=F32).astype(o_ref.dtype)


def matmul(x, w, out_dtype, tm_prefs=(1024, 512, 256, 128, 64), tn_prefs=(1024, 512, 256, 128)):
    m, k = x.shape
    n = w.shape[1]
    tm, tn = _pick(m, tm_prefs), _pick(n, tn_prefs)
    return pl.pallas_call(
        _mm_kernel,
        grid=(m // tm, n // tn),
        in_specs=[pl.BlockSpec((tm, k), lambda i, j: (i, 0)),
                  pl.BlockSpec((k, tn), lambda i, j: (0, j))],
        out_specs=pl.BlockSpec((tm, tn), lambda i, j: (i, j)),
        out_shape=jax.ShapeDtypeStruct((m, n), out_dtype),
        compiler_params=_cparams(("arbitrary", "arbitrary")),
        name="matmul",
    )(x, w)


def _mm_glu_kernel(x_ref, wa_ref, wb_ref, ba_ref, bb_ref, o_ref, *, act):
    x = x_ref[...]
    a = jnp.dot(x, wa_ref[...], preferred_element_type=F32) + ba_ref[...]
    b = jnp.dot(x, wb_ref[...], preferred_element_type=F32) + bb_ref[...]
    if act == "swiglu":
        r = _silu(a) * b
    else:
        r = a * _sigmoid(b)
    o_ref[...] = r.astype(o_ref.dtype)


def matmul_glu(x, w, bias, act, tm_prefs=(1024, 512, 256, 128, 64), tn_prefs=(512, 256, 128)):
    m, k = x.shape
    n = w.shape[1] // 2
    tm, tn = _pick(m, tm_prefs), _pick(n, tn_prefs)
    nb = n // tn
    return pl.pallas_call(
        functools.partial(_mm_glu_kernel, act=act),
        grid=(m // tm, nb),
        in_specs=[pl.BlockSpec((tm, k), lambda i, j: (i, 0)),
                  pl.BlockSpec((k, tn), lambda i, j: (0, j)),
                  pl.BlockSpec((k, tn), lambda i, j: (0, j + nb)),
                  pl.BlockSpec((1, tn), lambda i, j: (0, j)),
                  pl.BlockSpec((1, tn), lambda i, j: (0, j + nb))],
        out_specs=pl.BlockSpec((tm, tn), lambda i, j: (i, j)),
        out_shape=jax.ShapeDtypeStruct((m, n), BF16),
        compiler_params=_cparams(("arbitrary", "arbitrary")),
        name="matmul_" + act,
    )(x, w, w, bias, bias)


EPI_ROWS = 64


def _mm_res_kernel(a_ref, w_ref, b_ref, x_ref, nwp_ref, g_ref, nwn_ref, sc_ref, sh_ref,
                   xo_ref, *maybe_ho_ref, nk):
    k = pl.program_id(1)
    part = jnp.dot(a_ref[...], w_ref[...], preferred_element_type=F32)

    @pl.when(k == 0)
    def _():
        xo_ref[...] = part

    @pl.when(k > 0)
    def _():
        xo_ref[...] += part

    @pl.when(k == nk - 1)
    def _():
        def rows_step(r, carry):
            rows = pl.ds(pl.multiple_of(r * EPI_ROWS, EPI_ROWS), EPI_ROWS)
            y = xo_ref[rows, :] + b_ref[...]
            xn = x_ref[rows, :] + (1.0 + g_ref[...]) * (_rms(y) * nwp_ref[...])
            xo_ref[rows, :] = xn
            if maybe_ho_ref:
                hn = _rms(xn) * nwn_ref[...] * (1.0 + sc_ref[...]) + sh_ref[...]
                maybe_ho_ref[0][rows, :] = hn.astype(BF16)
            return carry

        lax.fori_loop(0, xo_ref.shape[0] // EPI_ROWS, rows_step, 0)


def matmul_residual(a, w, bias, x, nw_post, gate, nw_next, sc_next, sh_next, emit_next,
                    tm_prefs=(512, 256, 128, 64), tk_prefs=(512, 256, 128)):
    m, kdim = a.shape
    d = w.shape[1]
    tm, tk = _pick(m, tm_prefs), _pick(kdim, tk_prefs)
    nk = kdim // tk
    row = pl.BlockSpec((1, d), lambda i, k: (0, 0))
    tile = pl.BlockSpec((tm, d), lambda i, k: (i, 0))
    out_shape = [jax.ShapeDtypeStruct((m, d), F32)]
    out_specs = [tile]
    if emit_next:
        out_shape.append(jax.ShapeDtypeStruct((m, d), BF16))
        out_specs.append(tile)
    res = pl.pallas_call(
        functools.partial(_mm_res_kernel, nk=nk),
        grid=(m // tm, nk),
        in_specs=[pl.BlockSpec((tm, tk), lambda i, k: (i, k)),
                  pl.BlockSpec((tk, d), lambda i, k: (k, 0)),
                  row, tile, row, row, row, row, row],
        out_specs=out_specs,
        out_shape=out_shape,
        compiler_params=_cparams(("arbitrary", "arbitrary")),
        name="matmul_residual",
    )(a, w, bias, x, nw_post, gate, nw_next, sc_next, sh_next)
    return (res[0], res[1]) if emit_next else (res[0], None)


CONV_ROWS = 16


def _halo_specs(tm, tc, t):
    per = tm // HALO
    last = t // HALO - 1
    return [pl.BlockSpec((HALO, tc), lambda i, j: (jnp.maximum(i * per - 1, 0), j)),
            pl.BlockSpec((tm, tc), lambda i, j: (i, j)),
            pl.BlockSpec((HALO, tc), lambda i, j: (jnp.minimum((i + 1) * per, last), j))]


def _fill_ext(up_ref, uc_ref, un_ref, ext_ref):
    i, n = pl.program_id(0), pl.num_programs(0)
    tm = uc_ref.shape[0]
    ext_ref[0:HALO, :] = jnp.where(i > 0, up_ref[...].astype(F32), 0.0)
    ext_ref[HALO:HALO + tm, :] = uc_ref[...].astype(F32)
    ext_ref[HALO + tm:HALO + tm + HALO, :] = jnp.where(i < n - 1, un_ref[...].astype(F32), 0.0)


SUBLANES = 8


def _conv_rows(ext_ref, w_ref, base, width, cols=slice(None)):
    pad = width // 2
    lo = (HALO - pad) // SUBLANES * SUBLANES
    hi = -(-(HALO + pad + CONV_ROWS) // SUBLANES) * SUBLANES
    win = ext_ref[pl.ds(pl.multiple_of(base + lo, SUBLANES), hi - lo), cols]
    acc = None
    for j in range(width):
        off = HALO - pad + j - lo
        term = win[off:off + CONV_ROWS, :] * w_ref[j:j + 1, cols]
        acc = term if acc is None else acc + term
    return acc


CONF_COLS = 512


def _conf_conv_kernel(up_ref, uc_ref, un_ref, w_ref, b_ref, lnw_ref, lnb_ref, o_ref, ext_ref, cv_ref, *, width):
    _fill_ext(up_ref, uc_ref, un_ref, ext_ref)
    d = uc_ref.shape[1]

    def step(r, carry):
        base = pl.multiple_of(r * CONV_ROWS, CONV_ROWS)
        for c0 in range(0, d, CONF_COLS):
            cols = slice(c0, c0 + CONF_COLS)
            cv_ref[:, cols] = _conv_rows(ext_ref, w_ref, base, width, cols) + b_ref[:, cols]
        u = cv_ref[...]
        xc = u - jnp.mean(u, axis=-1, keepdims=True)
        y = xc * lax.rsqrt(jnp.mean(xc * xc, axis=-1, keepdims=True) + LN_EPS)
        y = y * lnw_ref[...] + lnb_ref[...]
        o_ref[pl.ds(base, CONV_ROWS), :] = _silu(y).astype(o_ref.dtype)
        return carry

    lax.fori_loop(0, uc_ref.shape[0] // CONV_ROWS, step, 0)


def conformer_conv_ln(u, dw_w, dw_b, ln_w, ln_b):
    t, d = u.shape
    width = dw_w.shape[0]
    assert width // 2 <= HALO and d % CONF_COLS == 0
    tm = _pick(t, (256, 128, 64, 32, 16))
    row = pl.BlockSpec((1, d), lambda i, j: (0, 0))
    return pl.pallas_call(
        functools.partial(_conf_conv_kernel, width=width),
        grid=(t // tm, 1),
        in_specs=_halo_specs(tm, d, t) + [pl.BlockSpec((width, d), lambda i, j: (0, 0)), row, row, row],
        out_specs=pl.BlockSpec((tm, d), lambda i, j: (i, 0)),
        out_shape=jax.ShapeDtypeStruct((t, d), BF16),
        scratch_shapes=[pltpu.VMEM((tm + 2 * HALO, d), F32), pltpu.VMEM((CONV_ROWS, d), F32)],
        compiler_params=_cparams(("arbitrary", "arbitrary")),
        name="conformer_conv_ln",
    )(u, u, u, dw_w, dw_b, ln_w, ln_b)


def _qkv_conv_kernel(up_ref, uc_ref, un_ref, w_ref, o_ref, ext_ref, *, width, q_tiles, qk_tiles, q_scale):
    _fill_ext(up_ref, uc_ref, un_ref, ext_ref)
    j = pl.program_id(1)
    nsteps = uc_ref.shape[0] // CONV_ROWS
    tc = uc_ref.shape[1]

    @pl.when(j < qk_tiles)
    def _():
        scale = jnp.where(j < q_tiles, q_scale, 1.0).astype(F32)

        def step(r, carry):
            base = pl.multiple_of(r * CONV_ROWS, CONV_ROWS)
            y = _silu(_conv_rows(ext_ref, w_ref, base, width))
            for h in range(tc // HEAD_DIM):
                seg = y[:, h * HEAD_DIM:(h + 1) * HEAD_DIM]
                inv = lax.rsqrt(jnp.sum(seg * seg, axis=-1, keepdims=True) + L2_EPS) * scale
                o_ref[pl.ds(base, CONV_ROWS), h * HEAD_DIM:(h + 1) * HEAD_DIM] = (seg * inv).astype(o_ref.dtype)
            return carry

        lax.fori_loop(0, nsteps, step, 0)

    @pl.when(j >= qk_tiles)
    def _():
        def step(r, carry):
            base = pl.multiple_of(r * CONV_ROWS, CONV_ROWS)
            y = _silu(_conv_rows(ext_ref, w_ref, base, width))
            o_ref[pl.ds(base, CONV_ROWS), :] = y.astype(o_ref.dtype)
            return carry

        lax.fori_loop(0, nsteps, step, 0)


def qkv_conv(proj, conv_w, qk_dim):
    t = proj.shape[0]
    width, cdim = conv_w.shape
    tm = _pick(t, (512, 256, 128, 64, 32, 16))
    tc = _pick(qk_dim, (1024, 512, 256, 128))
    return pl.pallas_call(
        functools.partial(_qkv_conv_kernel, width=width, q_tiles=qk_dim // tc,
                          qk_tiles=2 * qk_dim // tc, q_scale=HEAD_DIM ** -0.5),
        grid=(t // tm, cdim // tc),
        in_specs=_halo_specs(tm, tc, t) + [pl.BlockSpec((width, tc), lambda i, j: (0, j))],
        out_specs=pl.BlockSpec((tm, tc), lambda i, j: (i, j)),
        out_shape=jax.ShapeDtypeStruct((t, cdim), BF16),
        scratch_shapes=[pltpu.VMEM((tm + 2 * HALO, tc), F32)],
        compiler_params=_cparams(("arbitrary", "arbitrary")),
        name="qkv_conv",
    )(proj, proj, proj, conv_w)


def _split3(x):
    hi = x.astype(BF16)
    r = x - hi.astype(F32)
    mid = r.astype(BF16)
    lo = (r - mid.astype(F32)).astype(BF16)
    return hi, mid, lo


def _gate_kernel(ba_ref, alog_ref, dt_ref, o_ref, *, n_heads):
    tm = ba_ref.shape[0]
    ri = lax.broadcasted_iota(jnp.int32, (CHUNK, CHUNK), 0)
    ci = lax.broadcasted_iota(jnp.int32, (CHUNK, CHUNK), 1)
    tril = jnp.where(ri >= ci, 1.0, 0.0).astype(BF16)
    triu = jnp.where(ri <= ci, 1.0, 0.0).astype(BF16)
    lane = lax.broadcasted_iota(jnp.int32, (CHUNK, 4 * n_heads), 1)
    for c in range(tm // CHUNK):
        ba = ba_ref[c * CHUNK:(c + 1) * CHUNK, :]
        beta = _sigmoid(ba)
        x = ba + dt_ref[...]
        softplus = jnp.maximum(x, 0.0) + jnp.log(1.0 + jnp.exp(-jnp.abs(x)))
        g = -jnp.exp(alog_ref[...]) * softplus
        pre = jnp.zeros_like(g)
        suf = jnp.zeros_like(g)
        for part in _split3(g):
            pre = pre + jnp.dot(tril, part, preferred_element_type=F32)
            suf = suf + jnp.dot(triu, part, preferred_element_type=F32)
        cum = jnp.where(lane < 3 * n_heads, pre, suf)
        o_ref[c * CHUNK:(c + 1) * CHUNK, :] = jnp.where(lane < 2 * n_heads, beta, cum)


def gdn_gates(ba, a_log, dt_bias):
    t, n = ba.shape
    nv = n // 4
    tm = _pick(t, (512, 256, 128, 64))
    zeros = jnp.zeros((2 * nv,), F32)
    alog_row = jnp.concatenate([zeros, a_log.reshape(-1).astype(F32)]).reshape(1, n)
    dt_row = jnp.concatenate([zeros, dt_bias.reshape(-1).astype(F32)]).reshape(1, n)
    row = pl.BlockSpec((1, n), lambda i: (0, 0))
    return pl.pallas_call(
        functools.partial(_gate_kernel, n_heads=nv),
        grid=(t // tm,),
        in_specs=[pl.BlockSpec((tm, n), lambda i: (i, 0)), row, row],
        out_specs=pl.BlockSpec((tm, n), lambda i: (i, 0)),
        out_shape=jax.ShapeDtypeStruct((t, n), F32),
        compiler_params=_cparams(("arbitrary",)),
        name="gdn_gates",
    )(ba, alog_row, dt_row)


def _dot(a, b):
    return jnp.dot(a.astype(BF16), b.astype(BF16), preferred_element_type=F32)


def _dot_nt(a, b):
    return lax.dot_general(a.astype(BF16), b.astype(BF16), (((1,), (1,)), ((), ())),
                           preferred_element_type=F32)


def _dot_tn(a, b):
    return lax.dot_general(a.astype(BF16), b.astype(BF16), (((0,), (0,)), ((), ())),
                           preferred_element_type=F32)


def _inv_unit_triangular(l, ri, ci):
    same16 = jnp.right_shift(ri, 4) == jnp.right_shift(ci, 4)
    same32 = jnp.right_shift(ri, 5) == jnp.right_shift(ci, 5)
    eye = jnp.where(ri == ci, 1.0, 0.0)
    l16 =jnp.where(same16, l, 0.0)
    c32 = jnp.where(jnp.logical_and(same32, jnp.logical_not(same16)), l, 0.0)
    c64 = jnp.where(same32, 0.0, l)
    p = eye - l16
    sq = _dot(l16, l16)
    for _ in range(3):
        p = p + _dot(p, sq)
        sq = _dot(sq, sq)
    y = p - _dot(_dot(p, c32), p)
    return y - _dot(_dot(y, c64), y)


def _gdn_chunk(q, k, v, beta_row, g_row, glast_row, s, lower):
    c = q.shape[0]
    ri = lax.broadcasted_iota(jnp.int32, (c, c), 0)
    ci = lax.broadcasted_iota(jnp.int32, (c, c), 1)
    eye = ri == ci
    incl = (ri >= ci) if lower else (ri <= ci)
    strict = (ri > ci) if lower else (ri < ci)
    g_col = jnp.sum(jnp.where(eye, g_row, 0.0), axis=1, keepdims=True)
    beta_col = jnp.sum(jnp.where(eye, beta_row, 0.0), axis=1, keepdims=True)
    decay = jnp.exp(jnp.where(incl, g_col - g_row, -jnp.inf))
    kk = _dot_nt(k, k)
    qk = _dot_nt(q, k)
    l = jnp.where(strict, kk * decay * beta_col, 0.0)
    a = qk * decay
    t = _inv_unit_triangular(l, ri, ci)
    u = _dot(t * beta_row, v)
    w = _dot(t * (beta_row * jnp.exp(g_row)), k)
    v_new = u - _dot(w, s)
    o = jnp.exp(g_col) * _dot(q, s) + _dot(a, v_new)
    k_tail = k.astype(F32) * jnp.exp(glast_row - g_col)
    s_new = s * jnp.exp(glast_row) + _dot_tn(k_tail, v_new)
    return o, s_new


def _gdn_core_kernel(q_ref, k_ref, v_ref, z_ref, gf_ref, gb_ref, nw_ref, y_ref, s_ref, *, n_chunks):
    s_ref[...] = jnp.zeros_like(s_ref)
    half = n_chunks // 2
    dv = HEAD_DIM

    def partials(j):
        outs = []
        for d, (g_ref, lower) in enumerate(((gf_ref, True), (gb_ref, False))):
            cidx = j if d == 0 else n_chunks - 1 - j
            rows = pl.ds(pl.multiple_of(cidx * CHUNK, CHUNK), CHUNK)
            q = q_ref[rows, :]
            k = k_ref[rows, :]
            gates = g_ref[cidx]
            for vh in range(2):
                o, s_new = _gdn_chunk(q, k, v_ref[rows, vh * dv:(vh + 1) * dv],
                                      gates[vh:vh + 1, 0:CHUNK], gates[2 + vh:3 + vh, 0:CHUNK],
                                      gates[4 + vh:5 + vh, :], s_ref[2 * d + vh], lower)
                s_ref[2 * d + vh] = s_new
                outs.append((rows, vh, o))
        return outs

    def first_visit(j, carry):
        for rows, vh, o in partials(j):
            y_ref[rows, vh * dv:(vh + 1) * dv] = o.astype(y_ref.dtype)
        return carry

    def second_visit(j, carry):
        for rows, vh, o in partials(j):
            cols = slice(vh * dv, (vh + 1) * dv)
            tot = y_ref[rows, cols].astype(F32) + o
            gate = _silu(z_ref[rows, cols].astype(F32))
            y_ref[rows, cols] = (_rms(tot) * nw_ref[...] * gate).astype(y_ref.dtype)
        return carry

    lax.fori_loop(0, half, first_visit, 0)
    lax.fori_loop(half, n_chunks, second_visit, 0)


def gdn_core(qkv, proj, gates, norm_w, n_qk, n_v):
    t = qkv.shape[0]
    assert n_v == 2 * n_qk and t % (2 * CHUNK) == 0
    n_chunks = t // CHUNK
    dk = dv = HEAD_DIM
    qk_dim, v_dim = n_qk * dk, n_v * dv
    beta = gates[:, :2 * n_v].reshape(n_chunks, CHUNK, 2, n_qk, 2)
    cum = gates[:, 2 * n_v:].reshape(n_chunks, CHUNK, 2, n_qk, 2)
    rowform = lambda a: jnp.pad(a.transpose(2, 3, 0, 4, 1), ((0, 0),) * 4 + ((0, CHUNK),))
    last = jnp.stack([cum[:, CHUNK - 1, 0], cum[:, 0, 1]])
    last = jnp.broadcast_to(last.transpose(0, 2, 1, 3)[..., None], (2, n_qk, n_chunks, 2, 2 * CHUNK))
    table = jnp.concatenate([rowform(beta), rowform(cum), last,
                             jnp.zeros((2, n_qk, n_chunks, 2, 2 * CHUNK), F32)], axis=3)
    gate_spec = lambda d: pl.BlockSpec((None, None, n_chunks, 8, 2 * CHUNK), lambda h: (d, h, 0, 0, 0))
    return pl.pallas_call(
        functools.partial(_gdn_core_kernel, n_chunks=n_chunks),
        grid=(n_qk,),
        in_specs=[pl.BlockSpec((t, dk), lambda h: (0, h)),
                  pl.BlockSpec((t, dk), lambda h: (0, n_qk + h)),
                  pl.BlockSpec((t, 2 * dv), lambda h: (0, qk_dim // dv + h)),
                  pl.BlockSpec((t, 2 * dv), lambda h: (0, (2 * qk_dim + v_dim) // (2 * dv) + h)),
                  gate_spec(0), gate_spec(1),
                  pl.BlockSpec((1, dv), lambda h: (0, 0))],
        out_specs=pl.BlockSpec((t, 2 * dv), lambda h: (0, h)),
        out_shape=jax.ShapeDtypeStruct((t, v_dim), BF16),
        scratch_shapes=[pltpu.VMEM((4, dk, dv), F32)],
        compiler_params=_cparams(("arbitrary",)),
        name="gdn_core",
    )(qkv, qkv, qkv, proj, table, table, norm_w.reshape(1, dv).astype(F32))


def kernel(x, c, ada_w, ada_b, norm_w, gdn_in_w, gdn_conv_w, gdn_A_log, gdn_dt_bias, gdn_norm_w, gdn_out_w, cf_pw1_w, cf_pw1_b, cf_dw_w, cf_dw_b, cf_ln_w, cf_ln_b, cf_pw2_w, cf_pw2_b, ffn_in_w, ffn_out_w):
    bsz, t, d = x.shape
    assert bsz == 1
    depth = ada_w.shape[0]
    n_v = gdn_A_log.shape[-1]
    n_qk = n_v // 2
    qk_dim, v_dim = n_qk * HEAD_DIM, n_v * HEAD_DIM
    conv_dim = 2 * qk_dim + v_dim
    row = lambda a: a.reshape(1, -1).astype(F32)
    zero_row = jnp.zeros((1, d), F32)

    mod = ada_modulation(c, ada_w, ada_b)[:, 0].reshape(depth, 6, d)
    xs = x[0]
    h = prenorm(xs, row(norm_w[0, 0]), row(mod[0, 1]), row(mod[0, 0]))
    for i in range(depth):
        j = i // N_MIXERS
        if i % N_MIXERS == 0:
            w_in = gdn_in_w[j].astype(BF16)
            proj = matmul(h, w_in[:, :conv_dim + v_dim], BF16)
            ba = matmul(h, w_in[:, conv_dim + v_dim:], F32)
            gates = gdn_gates(ba, gdn_A_log[j], gdn_dt_bias[j])
            qkv = qkv_conv(proj, gdn_conv_w[j].astype(F32), qk_dim)
            mix = gdn_core(qkv, proj, gates, gdn_norm_w[j], n_qk, n_v)
            w_out, b_out = gdn_out_w[j].astype(BF16), zero_row
        else:
            u = matmul_glu(h, cf_pw1_w[j].astype(BF16), row(cf_pw1_b[j]), "glu")
            mix = conformer_conv_ln(u, cf_dw_w[j].astype(F32), row(cf_dw_b[j]), row(cf_ln_w[j]), row(cf_ln_b[j]))
            w_out, b_out = cf_pw2_w[j].astype(BF16), row(cf_pw2_b[j])
        xs, h = matmul_residual(mix, w_out, b_out, xs, row(norm_w[i, 1]), row(mod[i, 2]),
                                row(norm_w[i, 2]), row(mod[i, 4]), row(mod[i, 3]), True)
        f = matmul_glu(h, ffn_in_w[i].astype(BF16), jnp.zeros((1, ffn_in_w.shape[2]), F32), "swiglu")
        last = i == depth - 1
        nxt = (zero_row, zero_row, zero_row) if last else (
            row(norm_w[i + 1, 0]), row(mod[i + 1, 1]), row(mod[i + 1, 0]))
        xs, h = matmul_residual(f, ffn_out_w[i].astype(BF16), zero_row, xs, row(norm_w[i, 3]),
                                row(mod[i, 5]), *nxt, not last)
    return xs[None]
```

```python
import functools

import jax
import jax.numpy as jnp
from jax import lax
from jax.experimental import pallas as pl
from jax.experimental.pallas import tpu as pltpu

F32 = jnp.float32
BF16 = jnp.bfloat16

HEAD_DIM = 128
N_MIXERS = 2
CHUNK = 64
RMS_EPS = 1e-6
LN_EPS = 1e-5
L2_EPS = 1e-6

V7X_VMEM_BYTES = 64 * 1024 * 1024
VMEM_LIMIT = V7X_VMEM_BYTES - 12 * 1024 * 1024
HALO = 16


def _cparams(sem):
    return pltpu.CompilerParams(dimension_semantics=sem, vmem_limit_bytes=VMEM_LIMIT)


def _pick(n, prefs):
    for p in prefs:
        if n % p == 0:
            return p
    return n


def _sigmoid(x):
    return 1.0 / (1.0 + jnp.exp(-x))


def _silu(x):
    return x * _sigmoid(x)


def _rms(y):
    return y * lax.rsqrt(jnp.mean(y * y, axis=-1, keepdims=True) + RMS_EPS)


def _ada_kernel(c_ref, w_ref, b_ref, o_ref):
    cond = _silu(c_ref[...])
    o_ref[...] = jnp.dot(cond.astype(BF16), w_ref[...].astype(BF16),
                         preferred_element_type=F32) + b_ref[...]


def ada_modulation(c, ada_w, ada_b):
    depth, d, n = ada_w.shape
    b = c.shape[0]
    assert b <= 8
    cp = jnp.zeros((8, d), F32).at[:b].set(c)
    tn = _pick(n, (1024, 512, 256, 128))
    out = pl.pallas_call(
        _ada_kernel,
        grid=(depth, n // tn),
        in_specs=[pl.BlockSpec((8, d), lambda l, j: (0, 0)),
                  pl.BlockSpec((None, d, tn), lambda l, j: (l, 0, j)),
                  pl.BlockSpec((None, 1, tn), lambda l, j: (l, 0, j))],
        out_specs=pl.BlockSpec((None, 8, tn), lambda l, j: (l, 0, j)),
        out_shape=jax.ShapeDtypeStruct((depth, 8, n), F32),
        compiler_params=_cparams(("arbitrary", "arbitrary")),
        name="ada_modulation",
    )(cp, ada_w, ada_b.reshape(depth, 1, n))
    return out[:, :b]


def _prenorm_kernel(x_ref, nw_ref, sc_ref, sh_ref, h_ref):
    y = _rms(x_ref[...])
    h_ref[...] = (y * nw_ref[...] * (1.0 + sc_ref[...]) + sh_ref[...]).astype(h_ref.dtype)


def prenorm(x, nw, sc, sh):
    t, d = x.shape
    tm = _pick(t, (512, 256, 128, 64, 8))
    row = pl.BlockSpec((1, d), lambda i: (0, 0))
    return pl.pallas_call(
        _prenorm_kernel,
        grid=(t // tm,),
        in_specs=[pl.BlockSpec((tm, d), lambda i: (i, 0)), row, row, row],
        out_specs=pl.BlockSpec((tm, d), lambda i: (i, 0)),
        out_shape=jax.ShapeDtypeStruct((t, d), BF16),
        compiler_params=_cparams(("arbitrary",)),
        name="prenorm",
    )(x, nw, sc, sh)


def _mm_kernel(x_ref, w_ref, o_ref):
    o_ref[...] = jnp.dot(x_ref[...], w_ref[...], preferred_element_type=F32).astype(o_ref.dtype)


def matmul(x, w, out_dtype, tm_prefs=(1024, 512, 256, 128, 64), tn_prefs=(1024, 512, 256, 128)):
    m, k = x.shape
    n = w.shape[1]
    tm, tn = _pick(m, tm_prefs), _pick(n, tn_prefs)
    return pl.pallas_call(
        _mm_kernel,
        grid=(m // tm, n // tn),
        in_specs=[pl.BlockSpec((tm, k), lambda i, j: (i, 0)),
                  pl.BlockSpec((k, tn), lambda i, j: (0, j))],
        out_specs=pl.BlockSpec((tm, tn), lambda i, j: (i, j)),
        out_shape=jax.ShapeDtypeStruct((m, n), out_dtype),
        compiler_params=_cparams(("arbitrary", "arbitrary")),
        name="matmul",
    )(x, w)


def _mm_glu_kernel(x_ref, wa_ref, wb_ref, ba_ref, bb_ref, o_ref, *, act):
    x = x_ref[...]
    a = jnp.dot(x, wa_ref[...], preferred_element_type=F32) + ba_ref[...]
    b = jnp.dot(x, wb_ref[...], preferred_element_type=F32) + bb_ref[...]
    if act == "swiglu":
        r = _silu(a) * b
    else:
        r = a * _sigmoid(b)
    o_ref[...] = r.astype(o_ref.dtype)


def matmul_glu(x, w, bias, act, tm_prefs=(1024, 512, 256, 128, 64), tn_prefs=(512, 256, 128)):
    m, k = x.shape
    n = w.shape[1] // 2
    tm, tn = _pick(m, tm_prefs), _pick(n, tn_prefs)
    nb = n // tn
    return pl.pallas_call(
        functools.partial(_mm_glu_kernel, act=act),
        grid=(m // tm, nb),
        in_specs=[pl.BlockSpec((tm, k), lambda i, j: (i, 0)),
                  pl.BlockSpec((k, tn), lambda i, j: (0, j)),
                  pl.BlockSpec((k, tn), lambda i, j: (0, j + nb)),
                  pl.BlockSpec((1, tn), lambda i, j: (0, j)),
                  pl.BlockSpec((1, tn), lambda i, j: (0, j + nb))],
        out_specs=pl.BlockSpec((tm, tn), lambda i, j: (i, j)),
        out_shape=jax.ShapeDtypeStruct((m, n), BF16),
        compiler_params=_cparams(("arbitrary", "arbitrary")),
        name="matmul_" + act,
    )(x, w, w, bias, bias)


EPI_ROWS = 32


def _mm_res_kernel(a_ref, w_ref, b_ref, x_ref, nwp_ref, g_ref, nwn_ref, sc_ref, sh_ref,
                   xo_ref, *rest, nn):
    *maybe_ho_ref, y_scr = rest
    j = pl.program_id(1)
    y_scr[j] = jnp.dot(a_ref[...], w_ref[...], preferred_element_type=F32)

    @pl.when(j == nn - 1)
    def _():
        tn = y_scr.shape[2]
        d = nn * tn
        col = lambda ref, jj: ref[:, jj * tn:(jj + 1) * tn]

        def rows_step(r, carry):
            rows = pl.ds(pl.multiple_of(r * EPI_ROWS, EPI_ROWS), EPI_ROWS)
            ys = [y_scr[jj, rows, :] + col(b_ref, jj) for jj in range(nn)]
            inv = lax.rsqrt(sum(jnp.sum(y * y, axis=-1, keepdims=True) for y in ys) / d + RMS_EPS)
            xns = [x_ref[rows, jj * tn:(jj + 1) * tn] + (1.0 + col(g_ref, jj)) * (ys[jj] * inv * col(nwp_ref, jj))
                   for jj in range(nn)]
            for jj in range(nn):
                xo_ref[rows, jj * tn:(jj + 1) * tn] = xns[jj]
            if maybe_ho_ref:
                inv2 = lax.rsqrt(sum(jnp.sum(v * v, axis=-1, keepdims=True) for v in xns) / d + RMS_EPS)
                for jj in range(nn):
                    hn = xns[jj] * inv2 * col(nwn_ref, jj) * (1.0 + col(sc_ref, jj)) + col(sh_ref, jj)
                    maybe_ho_ref[0][rows, jj * tn:(jj + 1) * tn] = hn.astype(BF16)
            return carry

        lax.fori_loop(0, xo_ref.shape[0] // EPI_ROWS, rows_step, 0)


def matmul_residual(a, w, bias, x, nw_post, gate, nw_next, sc_next, sh_next, emit_next,
                    tm_prefs=(512, 256, 128, 64), tn_prefs=(512, 256, 128)):
    m, kdim = a.shape
    d = w.shape[1]
    tm, tn = _pick(m, tm_prefs), _pick(d, tn_prefs)
    nn = d // tn
    row = pl.BlockSpec((1, d), lambda i, j: (0, 0))
    tile = pl.BlockSpec((tm, d), lambda i, j: (i, 0))
    out_shape = [jax.ShapeDtypeStruct((m, d), F32)]
    out_specs = [tile]
    if emit_next:
        out_shape.append(jax.ShapeDtypeStruct((m, d), BF16))
        out_specs.append(tile)
    res = pl.pallas_call(
        functools.partial(_mm_res_kernel, nn=nn),
        grid=(m // tm, nn),
        in_specs=[pl.BlockSpec((tm, kdim), lambda i, j: (i, 0)),
                  pl.BlockSpec((kdim, tn), lambda i, j: (0, j)),
                  row, tile, row, row, row, row, row],
        out_specs=out_specs,
        out_shape=out_shape,
        scratch_shapes=[pltpu.VMEM((nn, tm, tn), F32)],
        compiler_params=_cparams(("arbitrary", "arbitrary")),
        name="matmul_residual",
    )(a, w, bias, x, nw_post, gate, nw_next, sc_next, sh_next)
    return (res[0], res[1]) if emit_next else (res[0], None)


LANES = 128


def _halo_specs(tm, tc, t):
    per = tm // HALO
    last = t // HALO - 1
    return [pl.BlockSpec((HALO, tc), lambda i, j: (jnp.maximum(i * per - 1, 0), j)),
            pl.BlockSpec((tm, tc), lambda i, j: (i, j)),
            pl.BlockSpec((HALO, tc), lambda i, j: (jnp.minimum((i + 1) * per, last), j))]


def _fill_ext(up_ref, uc_ref, un_ref, ext_ref):
    i, n = pl.program_id(0), pl.num_programs(0)
    tm = uc_ref.shape[0]
    for s in range(ext_ref.shape[0]):
        cols = slice(s * LANES, (s + 1) * LANES)
        ext_ref[s, 0:HALO, :] = jnp.where(i > 0, up_ref[:, cols].astype(F32), 0.0)
        ext_ref[s, HALO:HALO + tm, :] = uc_ref[:, cols].astype(F32)
        ext_ref[s, HALO + tm:HALO + tm + HALO, :] = jnp.where(i < n - 1, un_ref[:, cols].astype(F32), 0.0)


def _conv_rows(ext_ref, w_ref, slab, base, rows, width):
    pad = width // 2
    cols = slice(slab * LANES, (slab + 1) * LANES)
    acc = None
    for j in range(width):
        term = ext_ref[slab, pl.ds(base + (HALO - pad + j), rows), :] * w_ref[j:j + 1, cols]
        acc = term if acc is None else acc + term
    return acc


CONF_ROWS = 64
LN_ROWS = 16


def _conf_conv_kernel(up_ref, uc_ref, un_ref, w_ref, b_ref, lnw_ref, lnb_ref, o_ref, ext_ref, cv_ref, *, width):
    _fill_ext(up_ref, uc_ref, un_ref, ext_ref)
    n_slabs = ext_ref.shape[0]
    d = n_slabs * LANES

    def step(r, carry):
        base = pl.multiple_of(r * CONF_ROWS, CONF_ROWS)
        for s in range(n_slabs):
            cols = slice(s * LANES, (s + 1) * LANES)
            cv_ref[:, cols] = _conv_rows(ext_ref, w_ref, s, base, CONF_ROWS, width) + b_ref[:, cols]
        for r0 in range(0, CONF_ROWS, LN_ROWS):
            u = cv_ref[r0:r0 + LN_ROWS, :]
            xc = u - jnp.mean(u, axis=-1, keepdims=True)
            y = xc * lax.rsqrt(jnp.mean(xc * xc, axis=-1, keepdims=True) + LN_EPS)
            y = y * lnw_ref[...] + lnb_ref[...]
            o_ref[pl.ds(base + r0, LN_ROWS), :] = _silu(y).astype(o_ref.dtype)
        return carry

    lax.fori_loop(0, uc_ref.shape[0] // CONF_ROWS, step, 0)


def conformer_conv_ln(u, dw_w, dw_b, ln_w, ln_b):
    t, d = u.shape
    width = dw_w.shape[0]
    assert width // 2 <= HALO and d % LANES == 0
    tm = _pick(t, (256, 128, 64))
    row = pl.BlockSpec((1, d), lambda i, j: (0, 0))
    return pl.pallas_call(
        functools.partial(_conf_conv_kernel, width=width),
        grid=(t // tm, 1),
        in_specs=_halo_specs(tm, d, t) + [pl.BlockSpec((width, d), lambda i, j: (0, 0)), row, row, row],
        out_specs=pl.BlockSpec((tm, d), lambda i, j: (i, 0)),
        out_shape=jax.ShapeDtypeStruct((t, d), BF16),
        scratch_shapes=[pltpu.VMEM((d // LANES, tm + 2 * HALO, LANES), F32), pltpu.VMEM((CONF_ROWS, d), F32)],
        compiler_params=_cparams(("arbitrary", "arbitrary")),
        name="conformer_conv_ln",
    )(u, u, u, dw_w, dw_b, ln_w, ln_b)


QKV_ROWS = 32


def _qkv_conv_kernel(up_ref, uc_ref, un_ref, w_ref, o_ref, ext_ref, *, width, q_tiles, qk_tiles, q_scale):
    _fill_ext(up_ref, uc_ref, un_ref, ext_ref)
    j = pl.program_id(1)
    is_qk = j < qk_tiles
    scale = jnp.where(j < q_tiles, q_scale, 1.0).astype(F32)

    def step(r, carry):
        base = pl.multiple_of(r * QKV_ROWS, QKV_ROWS)
        for s in range(ext_ref.shape[0]):
            y = _silu(_conv_rows(ext_ref, w_ref, s, base, QKV_ROWS, width))
            inv = lax.rsqrt(jnp.sum(y * y, axis=-1, keepdims=True) + L2_EPS) * scale
            y = jnp.where(is_qk, y * inv, y)
            o_ref[pl.ds(base, QKV_ROWS), s * LANES:(s + 1) * LANES] = y.astype(o_ref.dtype)
        return carry

    lax.fori_loop(0, uc_ref.shape[0] // QKV_ROWS, step, 0)


def qkv_conv(proj, conv_w, qk_dim):
    t = proj.shape[0]
    width, cdim = conv_w.shape
    assert HEAD_DIM == LANES and width // 2 <= HALO
    tm = _pick(t, (512, 256, 128, 64, 32))
    tc = _pick(qk_dim, (1024, 512, 256, 128))
    return pl.pallas_call(
        functools.partial(_qkv_conv_kernel, width=width, q_tiles=qk_dim // tc,
                          qk_tiles=2 * qk_dim // tc, q_scale=HEAD_DIM ** -0.5),
        grid=(t // tm, cdim // tc),
        in_specs=_halo_specs(tm, tc, t) + [pl.BlockSpec((width, tc), lambda i, j: (0, j))],
        out_specs=pl.BlockSpec((tm, tc), lambda i, j: (i, j)),
        out_shape=jax.ShapeDtypeStruct((t, cdim), BF16),
        scratch_shapes=[pltpu.VMEM((tc // LANES, tm + 2 * HALO, LANES), F32)],
        compiler_params=_cparams(("arbitrary", "arbitrary")),
        name="qkv_conv",
    )(proj, proj, proj, conv_w)


def _split3(x):
    hi = x.astype(BF16)
    r = x - hi.astype(F32)
    mid = r.astype(BF16)
    lo = (r - mid.astype(F32)).astype(BF16)
    return hi, mid, lo


def _gate_kernel(ba_ref, alog_ref, dt_ref, o_ref, *, n_heads):
    tm = ba_ref.shape[0]
    ri = lax.broadcasted_iota(jnp.int32, (CHUNK, CHUNK), 0)
    ci = lax.broadcasted_iota(jnp.int32, (CHUNK, CHUNK), 1)
    tril = jnp.where(ri >= ci, 1.0, 0.0).astype(BF16)
    triu = jnp.where(ri <= ci, 1.0, 0.0).astype(BF16)
    lane = lax.broadcasted_iota(jnp.int32, (CHUNK, 4 * n_heads), 1)
    for c in range(tm // CHUNK):
        ba = ba_ref[c * CHUNK:(c + 1) * CHUNK, :]
        beta = _sigmoid(ba)
        x = ba + dt_ref[...]
        softplus = jnp.maximum(x, 0.0) + jnp.log(1.0 + jnp.exp(-jnp.abs(x)))
        g = -jnp.exp(alog_ref[...]) * softplus
        pre = jnp.zeros_like(g)
        suf = jnp.zeros_like(g)
        for part in _split3(g):
            pre = pre + jnp.dot(tril, part, preferred_element_type=F32)
            suf = suf + jnp.dot(triu, part, preferred_element_type=F32)
        cum = jnp.where(lane < 3 * n_heads, pre, suf)
        o_ref[c * CHUNK:(c + 1) * CHUNK, :] = jnp.where(lane < 2 * n_heads, beta, cum)


def gdn_gates(ba, a_log, dt_bias):
    t, n = ba.shape
    nv = n // 4
    tm = _pick(t, (512, 256, 128, 64))
    zeros = jnp.zeros((2 * nv,), F32)
    alog_row = jnp.concatenate([zeros, a_log.reshape(-1).astype(F32)]).reshape(1, n)
    dt_row = jnp.concatenate([zeros, dt_bias.reshape(-1).astype(F32)]).reshape(1, n)
    row = pl.BlockSpec((1, n), lambda i: (0, 0))
    return pl.pallas_call(
        functools.partial(_gate_kernel, n_heads=nv),
        grid=(t // tm,),
        in_specs=[pl.BlockSpec((tm, n), lambda i: (i, 0)), row, row],
        out_specs=pl.BlockSpec((tm, n), lambda i: (i, 0)),
        out_shape=jax.ShapeDtypeStruct((t, n), F32),
        compiler_params=_cparams(("arbitrary",)),
        name="gdn_gates",
    )(ba, alog_row, dt_row)


def _bdot(a, b, ca=2, cb=1):
    return lax.dot_general(a.astype(BF16), b.astype(BF16), (((ca,), (cb,)), ((0,), (0,))),
                           preferred_element_type=F32)


N_STREAMS = 4


class _PackedMasks:
    def __init__(self):
        c, w = CHUNK, N_STREAMS * CHUNK
        ri = lax.broadcasted_iota(jnp.int32, (c, w), 0)
        li = lax.broadcasted_iota(jnp.int32, (c, w), 1)
        ci = jnp.bitwise_and(li, c - 1)
        fwd = li < (N_STREAMS // 2) * c
        self.eye = ri == ci
        bwd = jnp.logical_not(fwd)
        self.incl = jnp.logical_or(jnp.logical_and(fwd, ri >= ci), jnp.logical_and(bwd, ri <= ci))
        self.strict = jnp.logical_or(jnp.logical_and(fwd, ri > ci), jnp.logical_and(bwd, ri < ci))
        self.seg = [jnp.right_shift(li, 6) == s for s in range(N_STREAMS)]
        self.same16 = jnp.right_shift(ri, 4) == jnp.right_shift(ci, 4)
        self.same32 = jnp.right_shift(ri, 5) == jnp.right_shift(ci, 5)
        rb = lax.broadcasted_iota(jnp.int32, (w, w), 0)
        lb = lax.broadcasted_iota(jnp.int32, (w, w), 1)
        self.blockdiag = jnp.right_shift(rb, 6) == jnp.right_shift(lb, 6)


    def spread(self, cols):
        out = cols[-1]
        for s in range(N_STREAMS - 2, -1, -1):
            out = jnp.where(self.seg[s], cols[s], out)
        return out

    def columns(self, row):
        z = jnp.where(self.eye, row, 0.0)
        return [jnp.sum(jnp.where(self.seg[s], z, 0.0), axis=-1, keepdims=True) for s in range(N_STREAMS)]

    def bd(self, xp):
        return jnp.where(self.blockdiag, jnp.concatenate([xp] * N_STREAMS, axis=-2), 0.0).astype(BF16)


def _inv_unit_triangular_packed(lp, m):
    mm = lambda xp, yp: _bdot(xp, m.bd(yp))
    l16 = jnp.where(m.same16, lp, 0.0)
    c32 = jnp.where(jnp.logical_and(m.same32, jnp.logical_not(m.same16)), lp, 0.0)
    c64 = jnp.where(m.same32, 0.0, lp)
    p = jnp.where(m.eye, 1.0, 0.0) - l16
    sq = mm(l16, l16)
    for _ in range(2):
        both = mm(jnp.concatenate([p, sq], axis=-2), sq)
        p, sq = p + both[:, :CHUNK], both[:, CHUNK:]
    p = p + mm(p, sq)
    y = p - mm(mm(p, c32), p)
    return y - mm(mm(y, c64), y)


def _gdn_prep_kernel(q_ref, k_ref, v_ref, tab_ref, gl_ref, pr_ref, qo_ref, *, chunks):
    c, dk, g = CHUNK, HEAD_DIM, PREP_GROUP
    m = _PackedMasks()

    def group_step(i, carry):
        j0 = pl.multiple_of(i * g, g)
        rows = pl.ds(pl.multiple_of(i * (g * c), g * c), g * c)
        q = q_ref[rows, :].reshape(g, c, dk)
        k = k_ref[rows, :].reshape(g, c, dk)
        v = v_ref[rows, :].reshape(g, c, 2 * dk)
        tab, gl = tab_ref[pl.ds(j0, g)], gl_ref[pl.ds(j0, g)]
        beta_p, g_p = tab[:, 0:1, :], tab[:, 1:2, :]
        x = _bdot(jnp.concatenate([q, k], axis=1), jnp.concatenate([k, k], axis=1), 2, 2)
        qk_p = jnp.concatenate([x[:, :c], x[:, :c]], axis=2)
        kk_p = jnp.concatenate([x[:, c:], x[:, c:]], axis=2)
        g_cols = m.columns(g_p)
        g_cb = m.spread(g_cols)
        beta_cb = m.spread(m.columns(beta_p))
        decay = jnp.exp(jnp.where(m.incl, g_cb - g_p, -jnp.inf))
        l_p = jnp.where(m.strict, kk_p * decay * beta_cb, 0.0)
        a_p = qk_p * decay
        t_p = _inv_unit_triangular_packed(l_p, m)
        k_st = jnp.concatenate([k] * N_STREAMS, axis=1)
        v_st = jnp.concatenate([v[:, :, :dk], v[:, :, dk:]] * (N_STREAMS // 2), axis=1)
        u_st = _bdot(m.bd(t_p * beta_p), v_st)
        w_st = _bdot(m.bd(t_p * (beta_p * jnp.exp(g_p))), k_st)
        uw = jnp.concatenate([u_st, w_st], axis=2).astype(BF16)
        a_uw = _bdot(m.bd(a_p), uw)
        g_cst = jnp.concatenate(g_cols, axis=1)
        q_st = jnp.concatenate([q] * N_STREAMS, axis=1).astype(F32)
        q_prime = q_st * jnp.exp(g_cst) - a_uw[:, :, dk:]
        qo = jnp.concatenate([q_prime, a_uw[:, :, :dk]], axis=2).astype(qo_ref.dtype)
        gl_st = jnp.concatenate([jnp.broadcast_to(gl[:, s:s + 1, :], (g, c, dk)) for s in range(N_STREAMS)], axis=1)
        kt_st = (k_st.astype(F32) * jnp.exp(gl_st - g_cst)).astype(BF16)
        for s in range(N_STREAMS):
            seg = slice(s * c, (s + 1) * c)
            d, vh = divmod(s, 2)
            pr_ref[d, vh, pl.ds(j0, g)] = _bdot(kt_st[:, seg], uw[:, seg], 1, 1).astype(pr_ref.dtype)
            qo_ref[d, vh, rows, :] = qo[:, seg].reshape(g * c, 2 * dk)
        return carry

    lax.fori_loop(0, chunks // g, group_step, 0)


SCAN_HEADS = 4


def _gdn_scan_kernel(prf_ref, prb_ref, qof_ref, qob_ref, glf_ref, glb_ref, of_ref, ob_ref, s_ref, *, chunks):
    c, dk = CHUNK, HEAD_DIM

    @pl.when(pl.program_id(1) == 0)
    def _():
        s_ref[...] = jnp.zeros_like(s_ref)

    def chunk_step(j, carry):
        for d, (pr_ref, qo_ref, gl_ref, o_ref) in enumerate(((prf_ref, qof_ref, glf_ref, of_ref),
                                                             (prb_ref, qob_ref, glb_ref, ob_ref))):
            cj = j if d == 0 else chunks - 1 - j
            rows = pl.ds(pl.multiple_of(cj * c, c), c)
            for vh in range(SCAN_HEADS):
                pr, qo = pr_ref[vh, cj], qo_ref[vh, rows, :]
                s = s_ref[d * SCAN_HEADS + vh]
                x = jnp.dot(jnp.concatenate([pr[:, dk:], qo[:, :dk]], axis=0), s.astype(BF16),
                            preferred_element_type=F32)
                o_ref[rows, vh * dk:(vh + 1) * dk] = (x[dk:] + qo[:, dk:].astype(F32)).astype(o_ref.dtype)
                s_ref[d * SCAN_HEADS + vh] = s * jnp.exp(gl_ref[vh, cj]) + pr[:, :dk].astype(F32) - x[:dk]
        return carry

    lax.fori_loop(0, chunks, chunk_step, 0)


def _gated_norm_kernel(of_ref, ob_ref, z_ref, nw_ref, y_ref):
    o = of_ref[...].astype(F32) + ob_ref[...].astype(F32)
    gate = _silu(z_ref[...].astype(F32))
    for h in range(o.shape[1] // HEAD_DIM):
        cols = slice(h * HEAD_DIM, (h + 1) * HEAD_DIM)
        y_ref[:, cols] = (_rms(o[:, cols]) * nw_ref[...] * gate[:, cols]).astype(y_ref.dtype)


PREP_GROUP = 8
PREP_CHUNKS = 8
SCAN_CHUNKS = 8


def gdn_core(qkv, proj, gates, norm_w, n_qk, n_v):
    t = qkv.shape[0]
    c, dk = CHUNK, HEAD_DIM
    dv = dk
    assert n_v == 2 * n_qk and t % c == 0 and n_v % SCAN_HEADS == 0
    nc = t // c
    qk_dim, v_dim = n_qk * dk, n_v * dv
    beta = gates[:, :2 * n_v].reshape(nc, c, 2, n_qk, 2)
    cum = gates[:, 2 * n_v:].reshape(nc, c, 2, n_qk, 2)
    packed = lambda a: a.transpose(3, 0, 2, 4, 1).reshape(n_qk, nc, 1, N_STREAMS * c)
    tab = jnp.concatenate([packed(beta), packed(cum), jnp.zeros((n_qk, nc, 6, N_STREAMS * c), F32)], axis=2)
    last = jnp.stack([cum[:, c - 1, 0], cum[:, 0, 1]])
    gl_prep = last.transpose(2, 1, 0, 3).reshape(n_qk, nc, N_STREAMS, 1)
    gl_prep = jnp.broadcast_to(jnp.pad(gl_prep, ((0, 0), (0, 0), (0, 8 - N_STREAMS), (0, 0))), (n_qk, nc, 8, dk))
    gl_scan = jnp.broadcast_to(last.reshape(2, nc, n_v).transpose(0, 2, 1)[..., None, None], (2, n_v, nc, 1, dv))

    pc = PREP_CHUNKS
    assert nc % pc == 0 and pc % PREP_GROUP == 0
    rb = pc * c
    pr, qo = pl.pallas_call(
        functools.partial(_gdn_prep_kernel, chunks=pc),
        grid=(n_qk, nc // pc),
        in_specs=[pl.BlockSpec((rb, dk), lambda h, b: (b, h)),
                  pl.BlockSpec((rb, dk), lambda h, b: (b, n_qk + h)),
                  pl.BlockSpec((rb, 2 * dv), lambda h, b: (b, qk_dim // dv + h)),
                  pl.BlockSpec((None, pc, 8, N_STREAMS * c), lambda h, b: (h, b, 0, 0)),
                  pl.BlockSpec((None, pc, 8, dk), lambda h, b: (h, b, 0, 0))],
        out_specs=[pl.BlockSpec((2, 2, pc, dk, 2 * dv), lambda h, b: (0, h, b, 0, 0)),
                   pl.BlockSpec((2, 2, rb, 2 * dv), lambda h, b: (0, h, b, 0))],
        out_shape=[jax.ShapeDtypeStruct((2, n_v, nc, dk, 2 * dv), BF16),
                   jax.ShapeDtypeStruct((2, n_v, t, 2 * dv), BF16)],
        compiler_params=_cparams(("arbitrary", "arbitrary")),
        name="gdn_prep",
    )(qkv, qkv, qkv, tab, gl_prep)

    sc = _pick(nc, (SCAN_CHUNKS, 4, 2, 1))
    ng = nc // sc
    rs = sc * c
    fwd = lambda *tail: (lambda g, b: (0, g, b) + tail)
    bwd = lambda *tail: (lambda g, b: (1, g, ng - 1 - b) + tail)
    o_f, o_b = pl.pallas_call(
        functools.partial(_gdn_scan_kernel, chunks=sc),
        grid=(n_v // SCAN_HEADS, ng),
        in_specs=[pl.BlockSpec((None, SCAN_HEADS, sc, dk, 2 * dv), fwd(0, 0)),
                  pl.BlockSpec((None, SCAN_HEADS, sc, dk, 2 * dv), bwd(0, 0)),
                  pl.BlockSpec((None, SCAN_HEADS, rs, 2 * dv), fwd(0)),
                  pl.BlockSpec((None, SCAN_HEADS, rs, 2 * dv), bwd(0)),
                  pl.BlockSpec((None, SCAN_HEADS, sc, 1, dv), fwd(0, 0)),
                  pl.BlockSpec((None, SCAN_HEADS, sc, 1, dv), bwd(0, 0))],
        out_specs=[pl.BlockSpec((rs, SCAN_HEADS * dv), lambda g, b: (b, g)),
                   pl.BlockSpec((rs, SCAN_HEADS * dv), lambda g, b: (ng - 1 - b, g))],
        out_shape=[jax.ShapeDtypeStruct((t, v_dim), BF16)] * 2,
        scratch_shapes=[pltpu.VMEM((2 * SCAN_HEADS, dk, dv), F32)],
        compiler_params=_cparams(("arbitrary", "arbitrary")),
        name="gdn_scan",
    )(pr, pr, qo, qo, gl_scan, gl_scan)

    tm = _pick(t, (512, 256, 128, 64))
    tc = _pick(v_dim, (1024, 512, 256, 128))
    z_off = (2 * qk_dim + v_dim) // tc
    tile = pl.BlockSpec((tm, tc), lambda i, j: (i, j))
    return pl.pallas_call(
        _gated_norm_kernel,
        grid=(t // tm, v_dim // tc),
        in_specs=[tile, tile, pl.BlockSpec((tm, tc), lambda i, j: (i, z_off + j)),
                  pl.BlockSpec((1, dv), lambda i, j: (0, 0))],
        out_specs=tile,
        out_shape=jax.ShapeDtypeStruct((t, v_dim), BF16),
        compiler_params=_cparams(("arbitrary", "arbitrary")),
        name="gdn_gated_norm",
    )(o_f, o_b, proj, norm_w.reshape(1, dv).astype(F32))


def kernel(x, c, ada_w, ada_b, norm_w, gdn_in_w, gdn_conv_w, gdn_A_log, gdn_dt_bias, gdn_norm_w, gdn_out_w, cf_pw1_w, cf_pw1_b, cf_dw_w, cf_dw_b, cf_ln_w, cf_ln_b, cf_pw2_w, cf_pw2_b, ffn_in_w, ffn_out_w):
    bsz, t, d = x.shape
    assert bsz == 1
    depth = ada_w.shape[0]
    n_v = gdn_A_log.shape[-1]
    n_qk = n_v // 2
    qk_dim, v_dim = n_qk * HEAD_DIM, n_v * HEAD_DIM
    conv_dim = 2 * qk_dim + v_dim
    row = lambda a: a.reshape(1, -1).astype(F32)
    zero_row = jnp.zeros((1, d), F32)

    mod = ada_modulation(c, ada_w, ada_b)[:, 0].reshape(depth, 6, d)
    xs = x[0]
    h = prenorm(xs, row(norm_w[0, 0]), row(mod[0, 1]), row(mod[0, 0]))
    for i in range(depth):
        j = i // N_MIXERS
        if i % N_MIXERS == 0:
            w_in = gdn_in_w[j].astype(BF16)
            proj = matmul(h, w_in[:, :conv_dim + v_dim], BF16)
            ba = matmul(h, w_in[:, conv_dim + v_dim:], F32)
            gates = gdn_gates(ba, gdn_A_log[j], gdn_dt_bias[j])
            qkv = qkv_conv(proj, gdn_conv_w[j].astype(F32), qk_dim)
            mix = gdn_core(qkv, proj, gates, gdn_norm_w[j], n_qk, n_v)
            w_out, b_out = gdn_out_w[j].astype(BF16), zero_row
        else:
            u = matmul_glu(h, cf_pw1_w[j].astype(BF16), row(cf_pw1_b[j]), "glu")
            mix = conformer_conv_ln(u, cf_dw_w[j].astype(F32), row(cf_dw_b[j]), row(cf_ln_w[j]), row(cf_ln_b[j]))
            w_out, b_out = cf_pw2_w[j].astype(BF16), row(cf_pw2_b[j])
        xs, h = matmul_residual(mix, w_out, b_out, xs, row(norm_w[i, 1]), row(mod[i, 2]),
                                row(norm_w[i, 2]), row(mod[i, 4]), row(mod[i, 3]), True)
        f = matmul_glu(h, ffn_in_w[i].astype(BF16), jnp.zeros((1, ffn_in_w.shape[2]), F32), "swiglu")
        last = i == depth - 1
        nxt = (zero_row, zero_row, zero_row) if last else (
            row(norm_w[i + 1, 0]), row(mod[i + 1, 1]), row(mod[i + 1, 0]))
        xs, h = matmul_residual(f, ffn_out_w[i].astype(BF16), zero_row, xs, row(norm_w[i, 3]),
                                row(mod[i, 5]), *nxt, not last)
    return xs[None]
```

```python
import functools

import jax
import jax.numpy as jnp
from jax import lax
from jax.experimental import pallas as pl
from jax.experimental.pallas import tpu as pltpu

F32 = jnp.float32
BF16 = jnp.bfloat16

HEAD_DIM = 128
N_MIXERS = 2
CHUNK = 64
RMS_EPS = 1e-6
LN_EPS = 1e-5
L2_EPS = 1e-6

V7X_VMEM_BYTES = 64 * 1024 * 1024
VMEM_LIMIT = V7X_VMEM_BYTES - 12 * 1024 * 1024
HALO = 16


def _cparams(sem):
    return pltpu.CompilerParams(dimension_semantics=sem, vmem_limit_bytes=VMEM_LIMIT)


def _pick(n, prefs):
    for p in prefs:
        if n % p == 0:
            return p
    return n


def _sigmoid(x):
    return 1.0 / (1.0 + jnp.exp(-x))


def _silu(x):
    return x * _sigmoid(x)


def _rms(y):
    return y * lax.rsqrt(jnp.mean(y * y, axis=-1, keepdims=True) + RMS_EPS)


def _ada_kernel(c_ref, w_ref, b_ref, o_ref):
    cond = _silu(c_ref[...])
    o_ref[...] = jnp.dot(cond.astype(BF16), w_ref[...].astype(BF16),
                         preferred_element_type=F32) + b_ref[...]


def ada_modulation(c, ada_w, ada_b):
    depth, d, n = ada_w.shape
    b = c.shape[0]
    assert b <= 8
    cp = jnp.zeros((8, d), F32).at[:b].set(c)
    tn = _pick(n, (1024, 512, 256, 128))
    out = pl.pallas_call(
        _ada_kernel,
        grid=(depth, n // tn),
        in_specs=[pl.BlockSpec((8, d), lambda l, j: (0, 0)),
                  pl.BlockSpec((None, d, tn), lambda l, j: (l, 0, j)),
                  pl.BlockSpec((None, 1, tn), lambda l, j: (l, 0, j))],
        out_specs=pl.BlockSpec((None, 8, tn), lambda l, j: (l, 0, j)),
        out_shape=jax.ShapeDtypeStruct((depth, 8, n), F32),
        compiler_params=_cparams(("arbitrary", "arbitrary")),
        name="ada_modulation",
    )(cp, ada_w, ada_b.reshape(depth, 1, n))
    return out[:, :b]


def _prenorm_kernel(x_ref, nw_ref, sc_ref, sh_ref, h_ref):
    y = _rms(x_ref[...])
    h_ref[...] = (y * nw_ref[...] * (1.0 + sc_ref[...]) + sh_ref[...]).astype(h_ref.dtype)


def prenorm(x, nw, sc, sh):
    t, d = x.shape
    tm = _pick(t, (512, 256, 128, 64, 8))
    row = pl.BlockSpec((1, d), lambda i: (0, 0))
    return pl.pallas_call(
        _prenorm_kernel,
        grid=(t // tm,),
        in_specs=[pl.BlockSpec((tm, d), lambda i: (i, 0)), row, row, row],
        out_specs=pl.BlockSpec((tm, d), lambda i: (i, 0)),
        out_shape=jax.ShapeDtypeStruct((t, d), BF16),
        compiler_params=_cparams(("arbitrary",)),
        name="prenorm",
    )(x, nw, sc, sh)


def _mm_kernel(x_ref, w_ref, o_ref, wb_ref):
    @pl.when(pl.program_id(1) == 0)
    def _():
        wb_ref[...] = w_ref[...].astype(BF16)

    o_ref[...] = jnp.dot(x_ref[...], wb_ref[...], preferred_element_type=F32).astype(o_ref.dtype)


def matmul(x, w, layer, col0, n, out_dtype, tm_prefs=(1024, 512, 256, 128, 64), tn_prefs=(1024, 512, 256, 128)):
    m, k = x.shape
    tm, tn = _pick(m, tm_prefs), _pick(n, tn_prefs)
    assert col0 % tn == 0
    j0 = col0 // tn
    return pl.pallas_call(
        _mm_kernel,
        grid=(n // tn, m // tm),
        in_specs=[pl.BlockSpec((tm, k), lambda j, i: (i, 0)),
                  pl.BlockSpec((None, k, tn), lambda j, i: (layer, 0, j0 + j))],
        out_specs=pl.BlockSpec((tm, tn), lambda j, i: (i, j)),
        out_shape=jax.ShapeDtypeStruct((m, n), out_dtype),
        scratch_shapes=[pltpu.VMEM((k, tn), BF16)],
        compiler_params=_cparams(("arbitrary", "arbitrary")),
        name="matmul",
    )(x, w)


def _mm_glu_kernel(x_ref, wa_ref, wb_ref, ba_ref, bb_ref, o_ref, wab_ref, wbb_ref, *, act):
    @pl.when(pl.program_id(1) == 0)
    def _():
        wab_ref[...] = wa_ref[...].astype(BF16)
        wbb_ref[...] = wb_ref[...].astype(BF16)

    x = x_ref[...]
    a = jnp.dot(x, wab_ref[...], preferred_element_type=F32) + ba_ref[...]
    b = jnp.dot(x, wbb_ref[...], preferred_element_type=F32) + bb_ref[...]
    if act == "swiglu":
        r = _silu(a) * b
    else:
        r = a * _sigmoid(b)
    o_ref[...] = r.astype(o_ref.dtype)


def matmul_glu(x, w, layer, bias, act, tm_prefs=(1024, 512, 256, 128, 64), tn_prefs=(512, 256, 128)):
    m, k = x.shape
    n = w.shape[2] // 2
    tm, tn = _pick(m, tm_prefs), _pick(n, tn_prefs)
    nb = n // tn
    return pl.pallas_call(
        functools.partial(_mm_glu_kernel, act=act),
        grid=(nb, m // tm),
        in_specs=[pl.BlockSpec((tm, k), lambda j, i: (i, 0)),
                  pl.BlockSpec((None, k, tn), lambda j, i: (layer, 0, j)),
                  pl.BlockSpec((None, k, tn), lambda j, i: (layer, 0, j + nb)),
                  pl.BlockSpec((1, tn), lambda j, i: (0, j)),
                  pl.BlockSpec((1, tn), lambda j, i: (0, j + nb))],
        out_specs=pl.BlockSpec((tm, tn), lambda j, i: (i, j)),
        out_shape=jax.ShapeDtypeStruct((m, n), BF16),
        scratch_shapes=[pltpu.VMEM((k, tn), BF16), pltpu.VMEM((k, tn), BF16)],
        compiler_params=_cparams(("arbitrary", "arbitrary")),
        name="matmul_" + act,
    )(x, w, w, bias, bias)


EPI_ROWS = 32


def _mm_res_kernel(a_ref, w_ref, b_ref, x_ref, nwp_ref, g_ref, nwn_ref, sc_ref, sh_ref,
                   xo_ref, *rest, nn):
    *maybe_ho_ref, y_scr = rest
    j = pl.program_id(1)
    y_scr[j] = jnp.dot(a_ref[...], w_ref[...], preferred_element_type=F32)

    @pl.when(j == nn - 1)
    def _():
        tn = y_scr.shape[2]
        d = nn * tn
        col = lambda ref, jj: ref[:, jj * tn:(jj + 1) * tn]

        def rows_step(r, carry):
            rows = pl.ds(pl.multiple_of(r * EPI_ROWS, EPI_ROWS), EPI_ROWS)
            ys = [y_scr[jj, rows, :] + col(b_ref, jj) for jj in range(nn)]
            inv = lax.rsqrt(sum(jnp.sum(y * y, axis=-1, keepdims=True) for y in ys) / d + RMS_EPS)
            xns = [x_ref[rows, jj * tn:(jj + 1) * tn] + (1.0 + col(g_ref, jj)) * (ys[jj] * inv * col(nwp_ref, jj))
                   for jj in range(nn)]
            for jj in range(nn):
                xo_ref[rows, jj * tn:(jj + 1) * tn] = xns[jj]
            if maybe_ho_ref:
                inv2 = lax.rsqrt(sum(jnp.sum(v * v, axis=-1, keepdims=True) for v in xns) / d + RMS_EPS)
                for jj in range(nn):
                    hn = xns[jj] * inv2 * col(nwn_ref, jj) * (1.0 + col(sc_ref, jj)) + col(sh_ref, jj)
                    maybe_ho_ref[0][rows, jj * tn:(jj + 1) * tn] = hn.astype(BF16)
            return carry

        lax.fori_loop(0, xo_ref.shape[0] // EPI_ROWS, rows_step, 0, unroll=2)


def matmul_residual(a, w, bias, x, nw_post, gate, nw_next, sc_next, sh_next, emit_next,
                    tm_prefs=(512, 256, 128, 64), tn_prefs=(512, 256, 128)):
    m, kdim = a.shape
    d = w.shape[1]
    tm, tn = _pick(m, tm_prefs), _pick(d, tn_prefs)
    nn = d // tn
    row = pl.BlockSpec((1, d), lambda i, j: (0, 0))
    tile = pl.BlockSpec((tm, d), lambda i, j: (i, 0))
    out_shape = [jax.ShapeDtypeStruct((m, d), F32)]
    out_specs = [tile]
    if emit_next:
        out_shape.append(jax.ShapeDtypeStruct((m, d), BF16))
        out_specs.append(tile)
    res = pl.pallas_call(
        functools.partial(_mm_res_kernel, nn=nn),
        grid=(m // tm, nn),
        in_specs=[pl.BlockSpec((tm, kdim), lambda i, j: (i, 0)),
                  pl.BlockSpec((kdim, tn), lambda i, j: (0, j)),
                  row, tile, row, row, row, row, row],
        out_specs=out_specs,
        out_shape=out_shape,
        scratch_shapes=[pltpu.VMEM((nn, tm, tn), F32)],
        compiler_params=_cparams(("arbitrary", "arbitrary")),
        name="matmul_residual",
    )(a, w, bias, x, nw_post, gate, nw_next, sc_next, sh_next)
    return (res[0], res[1]) if emit_next else (res[0], None)


LANES = 128


def _halo_specs(tm, tc, t):
    per = tm // HALO
    last = t // HALO - 1
    return [pl.BlockSpec((HALO, tc), lambda i, j: (jnp.maximum(i * per - 1, 0), j)),
            pl.BlockSpec((tm, tc), lambda i, j: (i, j)),
            pl.BlockSpec((HALO, tc), lambda i, j: (jnp.minimum((i + 1) * per, last), j))]


def _fill_ext(up_ref, uc_ref, un_ref, ext_ref):
    i, n = pl.program_id(0), pl.num_programs(0)
    tm = uc_ref.shape[0]
    for s in range(ext_ref.shape[0]):
        cols = slice(s * LANES, (s + 1) * LANES)
        ext_ref[s, 0:HALO, :] = jnp.where(i > 0, up_ref[:, cols].astype(F32), 0.0)
        ext_ref[s, HALO:HALO + tm, :] = uc_ref[:, cols].astype(F32)
        ext_ref[s, HALO + tm:HALO + tm + HALO, :] = jnp.where(i < n - 1, un_ref[:, cols].astype(F32), 0.0)


def _conv_rows(ext_ref, w_ref, slab, base, rows, width):
    pad = width // 2
    cols = slice(slab * LANES, (slab + 1) * LANES)
    acc = None
    for j in range(width):
        term = ext_ref[slab, pl.ds(base + (HALO - pad + j), rows), :] * w_ref[j:j + 1, cols]
        acc = term if acc is None else acc + term
    return acc


CONF_ROWS = 64
LN_ROWS = 16


def _conf_conv_kernel(up_ref, uc_ref, un_ref, w_ref, b_ref, lnw_ref, lnb_ref, o_ref, ext_ref, cv_ref, *, width):
    _fill_ext(up_ref, uc_ref, un_ref, ext_ref)
    n_slabs = ext_ref.shape[0]
    d = n_slabs * LANES

    def step(r, carry):
        base = pl.multiple_of(r * CONF_ROWS, CONF_ROWS)
        for s in range(n_slabs):
            cols = slice(s * LANES, (s + 1) * LANES)
            cv_ref[:, cols] = _conv_rows(ext_ref, w_ref, s, base, CONF_ROWS, width) + b_ref[:, cols]
        for r0 in range(0, CONF_ROWS, LN_ROWS):
            u = cv_ref[r0:r0 + LN_ROWS, :]
            xc = u - jnp.mean(u, axis=-1, keepdims=True)
            y = xc * lax.rsqrt(jnp.mean(xc * xc, axis=-1, keepdims=True) + LN_EPS)
            y = y * lnw_ref[...] + lnb_ref[...]
            o_ref[pl.ds(base + r0, LN_ROWS), :] = _silu(y).astype(o_ref.dtype)
        return carry

    lax.fori_loop(0, uc_ref.shape[0] // CONF_ROWS, step, 0)


def conformer_conv_ln(u, dw_w, dw_b, ln_w, ln_b):
    t, d = u.shape
    width = dw_w.shape[0]
    assert width // 2 <= HALO and d % LANES == 0
    tm = _pick(t, (256, 128, 64))
    row = pl.BlockSpec((1, d), lambda i, j: (0, 0))
    return pl.pallas_call(
        functools.partial(_conf_conv_kernel, width=width),
        grid=(t // tm, 1),
        in_specs=_halo_specs(tm, d, t) + [pl.BlockSpec((width, d), lambda i, j: (0, 0)), row, row, row],
        out_specs=pl.BlockSpec((tm, d), lambda i, j: (i, 0)),
        out_shape=jax.ShapeDtypeStruct((t, d), BF16),
        scratch_shapes=[pltpu.VMEM((d // LANES, tm + 2 * HALO, LANES), F32), pltpu.VMEM((CONF_ROWS, d), F32)],
        compiler_params=_cparams(("arbitrary", "arbitrary")),
        name="conformer_conv_ln",
    )(u, u, u, dw_w, dw_b, ln_w, ln_b)


QKV_ROWS = 32


def _qkv_conv_kernel(up_ref, uc_ref, un_ref, w_ref, o_ref, ext_ref, *, width, q_tiles, qk_tiles, q_scale):
    _fill_ext(up_ref, uc_ref, un_ref, ext_ref)
    j = pl.program_id(1)
    is_qk = j < qk_tiles
    scale = jnp.where(j < q_tiles, q_scale, 1.0).astype(F32)

    def step(r, carry):
        base = pl.multiple_of(r * QKV_ROWS, QKV_ROWS)
        for s in range(ext_ref.shape[0]):
            y = _silu(_conv_rows(ext_ref, w_ref, s, base, QKV_ROWS, width))
            inv = lax.rsqrt(jnp.sum(y * y, axis=-1, keepdims=True) + L2_EPS) * scale
            y = jnp.where(is_qk, y * inv, y)
            o_ref[pl.ds(base, QKV_ROWS), s * LANES:(s + 1) * LANES] = y.astype(o_ref.dtype)
        return carry

    lax.fori_loop(0, uc_ref.shape[0] // QKV_ROWS, step, 0)


def qkv_conv(proj, conv_w, qk_dim):
    t = proj.shape[0]
    width, cdim = conv_w.shape
    assert HEAD_DIM == LANES and width // 2 <= HALO
    tm = _pick(t, (512, 256, 128, 64, 32))
    tc = _pick(qk_dim, (1024, 512, 256, 128))
    return pl.pallas_call(
        functools.partial(_qkv_conv_kernel, width=width, q_tiles=qk_dim // tc,
                          qk_tiles=2 * qk_dim // tc, q_scale=HEAD_DIM ** -0.5),
        grid=(t // tm, cdim // tc),
        in_specs=_halo_specs(tm, tc, t) + [pl.BlockSpec((width, tc), lambda i, j: (0, j))],
        out_specs=pl.BlockSpec((tm, tc), lambda i, j: (i, j)),
        out_shape=jax.ShapeDtypeStruct((t, cdim), BF16),
        scratch_shapes=[pltpu.VMEM((tc // LANES, tm + 2 * HALO, LANES), F32)],
        compiler_params=_cparams(("arbitrary", "arbitrary")),
        name="qkv_conv",
    )(proj, proj, proj, conv_w)


def _split3(x):
    hi = x.astype(BF16)
    r = x - hi.astype(F32)
    mid = r.astype(BF16)
    lo = (r - mid.astype(F32)).astype(BF16)
    return hi, mid, lo


def _gate_kernel(ba_ref, alog_ref, dt_ref, o_ref, *, n_heads):
    tm = ba_ref.shape[0]
    ri = lax.broadcasted_iota(jnp.int32, (CHUNK, CHUNK), 0)
    ci = lax.broadcasted_iota(jnp.int32, (CHUNK, CHUNK), 1)
    tril = jnp.where(ri >= ci, 1.0, 0.0).astype(BF16)
    triu = jnp.where(ri <= ci, 1.0, 0.0).astype(BF16)
    lane = lax.broadcasted_iota(jnp.int32, (CHUNK, 4 * n_heads), 1)
    for c in range(tm // CHUNK):
        ba = ba_ref[c * CHUNK:(c + 1) * CHUNK, :]
        beta = _sigmoid(ba)
        x = ba + dt_ref[...]
        softplus = jnp.maximum(x, 0.0) + jnp.log(1.0 + jnp.exp(-jnp.abs(x)))
        g = -jnp.exp(alog_ref[...]) * softplus
        pre = jnp.zeros_like(g)
        suf = jnp.zeros_like(g)
        for part in _split3(g):
            pre = pre + jnp.dot(tril, part, preferred_element_type=F32)
            suf = suf + jnp.dot(triu, part, preferred_element_type=F32)
        cum = jnp.where(lane < 3 * n_heads, pre, suf)
        o_ref[c * CHUNK:(c + 1) * CHUNK, :] = jnp.where(lane < 2 * n_heads, beta, cum)


def gdn_gates(ba, a_log, dt_bias):
    t, n = ba.shape
    nv = n // 4
    tm = _pick(t, (512, 256, 128, 64))
    zeros = jnp.zeros((2 * nv,), F32)
    alog_row = jnp.concatenate([zeros, a_log.reshape(-1).astype(F32)]).reshape(1, n)
    dt_row = jnp.concatenate([zeros, dt_bias.reshape(-1).astype(F32)]).reshape(1, n)
    row = pl.BlockSpec((1, n), lambda i: (0, 0))
    return pl.pallas_call(
        functools.partial(_gate_kernel, n_heads=nv),
        grid=(t // tm,),
        in_specs=[pl.BlockSpec((tm, n), lambda i: (i, 0)), row, row],
        out_specs=pl.BlockSpec((tm, n), lambda i: (i, 0)),
        out_shape=jax.ShapeDtypeStruct((t, n), F32),
        compiler_params=_cparams(("arbitrary",)),
        name="gdn_gates",
    )(ba, alog_row, dt_row)


def _bdot(a, b, ca=2, cb=1):
    return lax.dot_general(a.astype(BF16), b.astype(BF16), (((ca,), (cb,)), ((0,), (0,))),
                           preferred_element_type=F32)


N_STREAMS = 4


class _PackedMasks:
    def __init__(self):
        c, w = CHUNK, N_STREAMS * CHUNK
        ri = lax.broadcasted_iota(jnp.int32, (c, w), 0)
        li = lax.broadcasted_iota(jnp.int32, (c, w), 1)
        ci = jnp.bitwise_and(li, c - 1)
        fwd = li < (N_STREAMS // 2) * c
        self.eye = ri == ci
        bwd = jnp.logical_not(fwd)
        self.incl = jnp.logical_or(jnp.logical_and(fwd, ri >= ci), jnp.logical_and(bwd, ri <= ci))
        self.strict = jnp.logical_or(jnp.logical_and(fwd, ri > ci), jnp.logical_and(bwd, ri < ci))
        self.seg = [jnp.right_shift(li, 6) == s for s in range(N_STREAMS)]
        self.same16 = jnp.right_shift(ri, 4) == jnp.right_shift(ci, 4)
        self.same32 = jnp.right_shift(ri, 5) == jnp.right_shift(ci, 5)
        rb = lax.broadcasted_iota(jnp.int32, (w, w), 0)
        lb = lax.broadcasted_iota(jnp.int32, (w, w), 1)
        self.blockdiag = jnp.right_shift(rb, 6) == jnp.right_shift(lb, 6)


    def spread(self, cols):
        out = cols[-1]
        for s in range(N_STREAMS - 2, -1, -1):
            out = jnp.where(self.seg[s], cols[s], out)
        return out

    def columns(self, row):
        z = jnp.where(self.eye, row, 0.0)
        return [jnp.sum(jnp.where(self.seg[s], z, 0.0), axis=-1, keepdims=True) for s in range(N_STREAMS)]

    def bd(self, xp):
        return jnp.where(self.blockdiag, jnp.concatenate([xp] * N_STREAMS, axis=-2), 0.0).astype(BF16)


def _inv_unit_triangular_packed(lp, m):
    mm = lambda xp, yp: _bdot(xp, m.bd(yp))
    l16 = jnp.where(m.same16, lp, 0.0)
    c32 = jnp.where(jnp.logical_and(m.same32, jnp.logical_not(m.same16)), lp, 0.0)
    c64 = jnp.where(m.same32, 0.0, lp)
    p = jnp.where(m.eye, 1.0, 0.0) - l16
    sq = mm(l16, l16)
    for _ in range(2):
        both = mm(jnp.concatenate([p, sq], axis=-2), sq)
        p, sq = p + both[:, :CHUNK], both[:, CHUNK:]
    p = p + mm(p, sq)
    y = p - mm(mm(p, c32), p)
    return y - mm(mm(y, c64), y)


def _gdn_prep_kernel(q_ref, k_ref, v_ref, tab_ref, gl_ref, pr_ref, qo_ref, *, chunks):
    c, dk, g = CHUNK, HEAD_DIM, PREP_GROUP
    m = _PackedMasks()

    def group_step(i, carry):
        j0 = pl.multiple_of(i * g, g)
        rows = pl.ds(pl.multiple_of(i * (g * c), g * c), g * c)
        q = q_ref[rows, :].reshape(g, c, dk)
        k = k_ref[rows, :].reshape(g, c, dk)
        v = v_ref[rows, :].reshape(g, c, 2 * dk)
        tab, gl = tab_ref[pl.ds(j0, g)], gl_ref[pl.ds(j0, g)]
        beta_p, g_p = tab[:, 0:1, :], tab[:, 1:2, :]
        x = _bdot(jnp.concatenate([q, k], axis=1), jnp.concatenate([k, k], axis=1), 2, 2)
        qk_p = jnp.concatenate([x[:, :c], x[:, :c]], axis=2)
        kk_p = jnp.concatenate([x[:, c:], x[:, c:]], axis=2)
        g_cols = m.columns(g_p)
        g_cb = m.spread(g_cols)
        beta_cb = m.spread(m.columns(beta_p))
        decay = jnp.exp(jnp.where(m.incl, g_cb - g_p, -jnp.inf))
        l_p = jnp.where(m.strict, kk_p * decay * beta_cb, 0.0)
        a_p = qk_p * decay
        t_p = _inv_unit_triangular_packed(l_p, m)
        k_st = jnp.concatenate([k] * N_STREAMS, axis=1)
        v_st = jnp.concatenate([v[:, :, :dk], v[:, :, dk:]] * (N_STREAMS // 2), axis=1)
        g_cst = jnp.concatenate(g_cols, axis=1)
        eg_cst = jnp.exp(g_cst)
        vk = jnp.concatenate([v_st, (k_st.astype(F32) * eg_cst).astype(BF16)], axis=2)
        uw = _bdot(m.bd(t_p * beta_p), vk).astype(BF16)
        a_uw = _bdot(m.bd(a_p), uw)
        q_st = jnp.concatenate([q] * N_STREAMS, axis=1).astype(F32)
        q_prime = q_st * eg_cst - a_uw[:, :, dk:]
        qo = jnp.concatenate([q_prime, a_uw[:, :, :dk]], axis=2).astype(qo_ref.dtype)
        gl_st = jnp.concatenate([jnp.broadcast_to(gl[:, s:s + 1, :], (g, c, dk)) for s in range(N_STREAMS)], axis=1)
        kt_st = (k_st.astype(F32) * jnp.exp(gl_st - g_cst)).astype(BF16)
        for s in range(N_STREAMS):
            seg = slice(s * c, (s + 1) * c)
            d, vh = divmod(s, 2)
            pr_ref[d, vh, pl.ds(j0, g)] = _bdot(kt_st[:, seg], uw[:, seg], 1, 1).astype(pr_ref.dtype)
            qo_ref[d, vh, rows, :] = qo[:, seg].reshape(g * c, 2 * dk)
        return carry

    lax.fori_loop(0, chunks // g, group_step, 0)


SCAN_HEADS = 8


def _gdn_scan_kernel(prf_ref, prb_ref, qof_ref, qob_ref, glf_ref, glb_ref, of_ref, ob_ref, s_ref, *, chunks):
    c, dk = CHUNK, HEAD_DIM

    @pl.when(pl.program_id(1) == 0)
    def _():
        s_ref[...] = jnp.zeros_like(s_ref)

    def chunk_step(j, carry):
        for d, (pr_ref, qo_ref, gl_ref, o_ref) in enumerate(((prf_ref, qof_ref, glf_ref, of_ref),
                                                             (prb_ref, qob_ref, glb_ref, ob_ref))):
            cj = j if d == 0 else chunks - 1 - j
            rows = pl.ds(pl.multiple_of(cj * c, c), c)
            for pair in range(SCAN_HEADS // 2):
                vhs = (2 * pair, 2 * pair + 1)
                prs = [pr_ref[vh, cj] for vh in vhs]
                qos = [qo_ref[vh, rows, :] for vh in vhs]
                ss = [s_ref[d * SCAN_HEADS + vh] for vh in vhs]
                lhs = jnp.concatenate([jnp.concatenate([pr[:, dk:], qo[:, :dk]], axis=0)
                                       for pr, qo in zip(prs, qos)], axis=1)
                zero = jnp.zeros((dk, dk), BF16)
                s_bd = jnp.concatenate([jnp.concatenate([ss[0].astype(BF16), zero], axis=1),
                                        jnp.concatenate([zero, ss[1].astype(BF16)], axis=1)], axis=0)
                x = jnp.dot(lhs, s_bd, preferred_element_type=F32)
                for n, vh in enumerate(vhs):
                    xs = x[:, n * dk:(n + 1) * dk]
                    o_ref[rows, vh * dk:(vh + 1) * dk] = (xs[dk:] + qos[n][:, dk:].astype(F32)).astype(o_ref.dtype)
                    s_ref[d * SCAN_HEADS + vh] = (ss[n] * jnp.exp(gl_ref[vh, cj])
                                                  + prs[n][:, :dk].astype(F32) - xs[:dk])
        return carry

    lax.fori_loop(0, chunks, chunk_step, 0)


def _gated_norm_kernel(of_ref, ob_ref, z_ref, nw_ref, y_ref):
    o = of_ref[...].astype(F32) + ob_ref[...].astype(F32)
    gate = _silu(z_ref[...].astype(F32))
    for h in range(o.shape[1] // HEAD_DIM):
        cols = slice(h * HEAD_DIM, (h + 1) * HEAD_DIM)
        y_ref[:, cols] = (_rms(o[:, cols]) * nw_ref[...] * gate[:, cols]).astype(y_ref.dtype)


PREP_GROUP = 8
PREP_CHUNKS = 8
SCAN_CHUNKS = 8


def gdn_core(qkv, proj, gates, norm_w, n_qk, n_v):
    t = qkv.shape[0]
    c, dk = CHUNK, HEAD_DIM
    dv = dk
    assert n_v == 2 * n_qk and t % c == 0 and n_v % SCAN_HEADS == 0
    nc = t // c
    qk_dim, v_dim = n_qk * dk, n_v * dv
    beta = gates[:, :2 * n_v].reshape(nc, c, 2, n_qk, 2)
    cum = gates[:, 2 * n_v:].reshape(nc, c, 2, n_qk, 2)
    packed = lambda a: a.transpose(3, 0, 2, 4, 1).reshape(n_qk, nc, 1, N_STREAMS * c)
    tab = jnp.concatenate([packed(beta), packed(cum), jnp.zeros((n_qk, nc, 6, N_STREAMS * c), F32)], axis=2)
    last = jnp.stack([cum[:, c - 1, 0], cum[:, 0, 1]])
    gl_prep = last.transpose(2, 1, 0, 3).reshape(n_qk, nc, N_STREAMS, 1)
    gl_prep = jnp.broadcast_to(jnp.pad(gl_prep, ((0, 0), (0, 0), (0, 8 - N_STREAMS), (0, 0))), (n_qk, nc, 8, dk))
    gl_scan = jnp.broadcast_to(last.reshape(2, nc, n_v).transpose(0, 2, 1)[..., None, None], (2, n_v, nc, 1, dv))

    pc = PREP_CHUNKS
    assert nc % pc == 0 and pc % PREP_GROUP == 0
    rb = pc * c
    pr, qo = pl.pallas_call(
        functools.partial(_gdn_prep_kernel, chunks=pc),
        grid=(n_qk, nc // pc),
        in_specs=[pl.BlockSpec((rb, dk), lambda h, b: (b, h)),
                  pl.BlockSpec((rb, dk), lambda h, b: (b, n_qk + h)),
                  pl.BlockSpec((rb, 2 * dv), lambda h, b: (b, qk_dim // dv + h)),
                  pl.BlockSpec((None, pc, 8, N_STREAMS * c), lambda h, b: (h, b, 0, 0)),
                  pl.BlockSpec((None, pc, 8, dk), lambda h, b: (h, b, 0, 0))],
        out_specs=[pl.BlockSpec((2, 2, pc, dk, 2 * dv), lambda h, b: (0, h, b, 0, 0)),
                   pl.BlockSpec((2, 2, rb, 2 * dv), lambda h, b: (0, h, b, 0))],
        out_shape=[jax.ShapeDtypeStruct((2, n_v, nc, dk, 2 * dv), BF16),
                   jax.ShapeDtypeStruct((2, n_v, t, 2 * dv), BF16)],
        compiler_params=_cparams(("arbitrary", "arbitrary")),
        name="gdn_prep",
    )(qkv, qkv, qkv, tab, gl_prep)

    sc = _pick(nc, (SCAN_CHUNKS, 4, 2, 1))
    ng = nc // sc
    rs = sc * c
    fwd = lambda *tail: (lambda g, b: (0, g, b) + tail)
    bwd = lambda *tail: (lambda g, b: (1, g, ng - 1 - b) + tail)
    o_f, o_b = pl.pallas_call(
        functools.partial(_gdn_scan_kernel, chunks=sc),
        grid=(n_v // SCAN_HEADS, ng),
        in_specs=[pl.BlockSpec((None, SCAN_HEADS, sc, dk, 2 * dv), fwd(0, 0)),
                  pl.BlockSpec((None, SCAN_HEADS, sc, dk, 2 * dv), bwd(0, 0)),
                  pl.BlockSpec((None, SCAN_HEADS, rs, 2 * dv), fwd(0)),
                  pl.BlockSpec((None, SCAN_HEADS, rs, 2 * dv), bwd(0)),
                  pl.BlockSpec((None, SCAN_HEADS, sc, 1, dv), fwd(0, 0)),
                  pl.BlockSpec((None, SCAN_HEADS, sc, 1, dv), bwd(0, 0))],
        out_specs=[pl.BlockSpec((rs, SCAN_HEADS * dv), lambda g, b: (b, g)),
                   pl.BlockSpec((rs, SCAN_HEADS * dv), lambda g, b: (ng - 1 - b, g))],
        out_shape=[jax.ShapeDtypeStruct((t, v_dim), BF16)] * 2,
        scratch_shapes=[pltpu.VMEM((2 * SCAN_HEADS, dk, dv), F32)],
        compiler_params=_cparams(("arbitrary", "arbitrary")),
        name="gdn_scan",
    )(pr, pr, qo, qo, gl_scan, gl_scan)

    tm = _pick(t, (512, 256, 128, 64))
    tc = _pick(v_dim, (1024, 512, 256, 128))
    z_off = (2 * qk_dim + v_dim) // tc
    tile = pl.BlockSpec((tm, tc), lambda i, j: (i, j))
    return pl.pallas_call(
        _gated_norm_kernel,
        grid=(t // tm, v_dim // tc),
        in_specs=[tile, tile, pl.BlockSpec((tm, tc), lambda i, j: (i, z_off + j)),
                  pl.BlockSpec((1, dv), lambda i, j: (0, 0))],
        out_specs=tile,
        out_shape=jax.ShapeDtypeStruct((t, v_dim), BF16),
        compiler_params=_cparams(("arbitrary", "arbitrary")),
        name="gdn_gated_norm",
    )(o_f, o_b, proj, norm_w.reshape(1, dv).astype(F32))


def kernel(x, c, ada_w, ada_b, norm_w, gdn_in_w, gdn_conv_w, gdn_A_log, gdn_dt_bias, gdn_norm_w, gdn_out_w, cf_pw1_w, cf_pw1_b, cf_dw_w, cf_dw_b, cf_ln_w, cf_ln_b, cf_pw2_w, cf_pw2_b, ffn_in_w, ffn_out_w):
    bsz, t, d = x.shape
    assert bsz == 1
    depth = ada_w.shape[0]
    n_v = gdn_A_log.shape[-1]
    n_qk = n_v // 2
    qk_dim, v_dim = n_qk * HEAD_DIM, n_v * HEAD_DIM
    conv_dim = 2 * qk_dim + v_dim
    row = lambda a: a.reshape(1, -1).astype(F32)
    zero_row = jnp.zeros((1, d), F32)

    mod = ada_modulation(c, ada_w, ada_b)[:, 0].reshape(depth, 6, d)
    xs = x[0]
    h = prenorm(xs, row(norm_w[0, 0]), row(mod[0, 1]), row(mod[0, 0]))
    for i in range(depth):
        j = i // N_MIXERS
        if i % N_MIXERS == 0:
            proj = matmul(h, gdn_in_w, j, 0, conv_dim + v_dim, BF16)
            ba = matmul(h, gdn_in_w, j, conv_dim + v_dim, 4 * n_v, F32)
            gates = gdn_gates(ba, gdn_A_log[j], gdn_dt_bias[j])
            qkv = qkv_conv(proj, gdn_conv_w[j].astype(F32), qk_dim)
            mix = gdn_core(qkv, proj, gates, gdn_norm_w[j], n_qk, n_v)
            w_out, b_out = gdn_out_w[j].astype(BF16), zero_row
        else:
            u = matmul_glu(h, cf_pw1_w, j, row(cf_pw1_b[j]), "glu")
            mix = conformer_conv_ln(u, cf_dw_w[j].astype(F32), row(cf_dw_b[j]), row(cf_ln_w[j]), row(cf_ln_b[j]))
            w_out, b_out = cf_pw2_w[j].astype(BF16), row(cf_pw2_b[j])
        xs, h = matmul_residual(mix, w_out, b_out, xs, row(norm_w[i, 1]), row(mod[i, 2]),
                                row(norm_w[i, 2]), row(mod[i, 4]), row(mod[i, 3]), True)
        f = matmul_glu(h, ffn_in_w, i, jnp.zeros((1, ffn_in_w.shape[2]), F32), "swiglu")
        last = i == depth - 1
        nxt = (zero_row, zero_row, zero_row) if last else (
            row(norm_w[i + 1, 0]), row(mod[i + 1, 1]), row(mod[i + 1, 0]))
        xs, h = matmul_residual(f, ffn_out_w[i].astype(BF16), zero_row, xs, row(norm_w[i, 3]),
                                row(mod[i, 5]), *nxt, not last)
    return xs[None]
```

```python
import functools

import jax
import jax.numpy as jnp
from jax import lax
from jax.experimental import pallas as pl
from jax.experimental.pallas import tpu as pltpu

F32 = jnp.float32
BF16 = jnp.bfloat16

HEAD_DIM = 128
N_MIXERS = 2
CHUNK = 64
RMS_EPS = 1e-6
LN_EPS = 1e-5
L2_EPS = 1e-6

V7X_VMEM_BYTES = 64 * 1024 * 1024
VMEM_LIMIT = V7X_VMEM_BYTES - 12 * 1024 * 1024
HALO = 16


def _cparams(sem):
    return pltpu.CompilerParams(dimension_semantics=sem, vmem_limit_bytes=VMEM_LIMIT)


def _pick(n, prefs):
    for p in prefs:
        if n % p == 0:
            return p
    return n


def _sigmoid(x):
    return 1.0 / (1.0 + jnp.exp(-x))


def _silu(x):
    return x * _sigmoid(x)


def _rms(y):
    return y * lax.rsqrt(jnp.mean(y * y, axis=-1, keepdims=True) + RMS_EPS)


def _ada_kernel(c_ref, w_ref, b_ref, o_ref):
    cond = _silu(c_ref[...])
    o_ref[...] = jnp.dot(cond.astype(BF16), w_ref[...].astype(BF16),
                         preferred_element_type=F32) + b_ref[...]


def ada_modulation(c, ada_w, ada_b):
    depth, d, n = ada_w.shape
    b = c.shape[0]
    assert b <= 8
    cp = jnp.zeros((8, d), F32).at[:b].set(c)
    tn = _pick(n, (1024, 512, 256, 128))
    out = pl.pallas_call(
        _ada_kernel,
        grid=(depth, n // tn),
        in_specs=[pl.BlockSpec((8, d), lambda l, j: (0, 0)),
                  pl.BlockSpec((None, d, tn), lambda l, j: (l, 0, j)),
                  pl.BlockSpec((None, 1, tn), lambda l, j: (l, 0, j))],
        out_specs=pl.BlockSpec((None, 8, tn), lambda l, j: (l, 0, j)),
        out_shape=jax.ShapeDtypeStruct((depth, 8, n), F32),
        compiler_params=_cparams(("arbitrary", "arbitrary")),
        name="ada_modulation",
    )(cp, ada_w, ada_b.reshape(depth, 1, n))
    return out[:, :b]


def _prenorm_kernel(x_ref, nw_ref, sc_ref, sh_ref, h_ref):
    y = _rms(x_ref[...])
    h_ref[...] = (y * nw_ref[...] * (1.0 + sc_ref[...]) + sh_ref[...]).astype(h_ref.dtype)


def prenorm(x, nw, sc, sh):
    t, d = x.shape
    tm = _pick(t, (512, 256, 128, 64, 8))
    row = pl.BlockSpec((1, d), lambda i: (0, 0))
    return pl.pallas_call(
        _prenorm_kernel,
        grid=(t // tm,),
        in_specs=[pl.BlockSpec((tm, d), lambda i: (i, 0)), row, row, row],
        out_specs=pl.BlockSpec((tm, d), lambda i: (i, 0)),
        out_shape=jax.ShapeDtypeStruct((t, d), BF16),
        compiler_params=_cparams(("arbitrary",)),
        name="prenorm",
    )(x, nw, sc, sh)


def _mm_kernel(x_ref, w_ref, o_ref, wb_ref):
    @pl.when(pl.program_id(1) == 0)
    def _():
        wb_ref[...] = w_ref[...].astype(BF16)

    o_ref[...] = jnp.dot(x_ref[...], wb_ref[...], preferred_element_type=F32).astype(o_ref.dtype)


def matmul(x, w, layer, col0, n, out_dtype, tm_prefs=(1024, 512, 256, 128, 64), tn_prefs=(1024, 512, 256, 128)):
    m, k = x.shape
    tm, tn = _pick(m, tm_prefs), _pick(n, tn_prefs)
    assert col0 % tn == 0
    j0 = col0 // tn
    return pl.pallas_call(
        _mm_kernel,
        grid=(n // tn, m // tm),
        in_specs=[pl.BlockSpec((tm, k), lambda j, i: (i, 0)),
                  pl.BlockSpec((None, k, tn), lambda j, i: (layer, 0, j0 + j))],
        out_specs=pl.BlockSpec((tm, tn), lambda j, i: (i, j)),
        out_shape=jax.ShapeDtypeStruct((m, n), out_dtype),
        scratch_shapes=[pltpu.VMEM((k, tn), BF16)],
        compiler_params=_cparams(("arbitrary", "arbitrary")),
        name="matmul",
    )(x, w)


def _mm_glu_kernel(x_ref, wa_ref, wb_ref, ba_ref, bb_ref, o_ref, wab_ref, wbb_ref, *, act):
    @pl.when(pl.program_id(1) == 0)
    def _():
        wab_ref[...] = wa_ref[...].astype(BF16)
        wbb_ref[...] = wb_ref[...].astype(BF16)

    x = x_ref[...]
    a = jnp.dot(x, wab_ref[...], preferred_element_type=F32) + ba_ref[...]
    b = jnp.dot(x, wbb_ref[...], preferred_element_type=F32) + bb_ref[...]
    if act == "swiglu":
        r = _silu(a) * b
    else:
        r = a * _sigmoid(b)
    o_ref[...] = r.astype(o_ref.dtype)


def matmul_glu(x, w, layer, bias, act, tm_prefs=(1024, 512, 256, 128, 64), tn_prefs=(512, 256, 128)):
    m, k = x.shape
    n = w.shape[2] // 2
    tm, tn = _pick(m, tm_prefs), _pick(n, tn_prefs)
    nb = n // tn
    return pl.pallas_call(
        functools.partial(_mm_glu_kernel, act=act),
        grid=(nb, m // tm),
        in_specs=[pl.BlockSpec((tm, k), lambda j, i: (i, 0)),
                  pl.BlockSpec((None, k, tn), lambda j, i: (layer, 0, j)),
                  pl.BlockSpec((None, k, tn), lambda j, i: (layer, 0, j + nb)),
                  pl.BlockSpec((1, tn), lambda j, i: (0, j)),
                  pl.BlockSpec((1, tn), lambda j, i: (0, j + nb))],
        out_specs=pl.BlockSpec((tm, tn), lambda j, i: (i, j)),
        out_shape=jax.ShapeDtypeStruct((m, n), BF16),
        scratch_shapes=[pltpu.VMEM((k, tn), BF16), pltpu.VMEM((k, tn), BF16)],
        compiler_params=_cparams(("arbitrary", "arbitrary")),
        name="matmul_" + act,
    )(x, w, w, bias, bias)


EPI_ROWS = 32


def _mm_res_kernel(a_ref, w_ref, b_ref, x_ref, nwp_ref, g_ref, nwn_ref, sc_ref, sh_ref,
                   xo_ref, *rest, nn):
    *maybe_ho_ref, y_even, y_odd = rest
    i, j = pl.program_id(0), pl.program_id(1)
    n_tiles = pl.num_programs(0) - 1
    tm, tn = y_even.shape[1], y_even.shape[2]
    d = nn * tn
    rows_per_step = tm // nn
    col = lambda ref, jj: ref[:, jj * tn:(jj + 1) * tn]

    @pl.when(jnp.logical_and(i == 0, j == 0))
    def _():
        y_odd[...] = jnp.zeros_like(y_odd)

    def epilogue_slice(y_prev):
        for r in range(rows_per_step // EPI_ROWS):
            rows = pl.ds(pl.multiple_of(j * rows_per_step + r * EPI_ROWS, EPI_ROWS), EPI_ROWS)
            ys = [y_prev[jj, rows, :] + col(b_ref, jj) for jj in range(nn)]
            inv = lax.rsqrt(sum(jnp.sum(y * y, axis=-1, keepdims=True) for y in ys) / d + RMS_EPS)
            xns = [x_ref[rows, jj * tn:(jj + 1) * tn] + (1.0 + col(g_ref, jj)) * (ys[jj] * inv * col(nwp_ref, jj))
                   for jj in range(nn)]
            for jj in range(nn):
                xo_ref[rows, jj * tn:(jj + 1) * tn] = xns[jj]
            if maybe_ho_ref:
                inv2 = lax.rsqrt(sum(jnp.sum(v * v, axis=-1, keepdims=True) for v in xns) / d + RMS_EPS)
                for jj in range(nn):
                    hn = xns[jj] * inv2 * col(nwn_ref, jj) * (1.0 + col(sc_ref, jj)) + col(sh_ref, jj)
                    maybe_ho_ref[0][rows, jj * tn:(jj + 1) * tn] = hn.astype(BF16)

    def step(y_cur, y_prev):
        y_cur[j] = jnp.dot(a_ref[...], w_ref[...], preferred_element_type=F32)
        epilogue_slice(y_prev)

    even, busy = lax.rem(i, 2) == 0, i < n_tiles
    pl.when(jnp.logical_and(busy, even))(lambda: step(y_even, y_odd))
    pl.when(jnp.logical_and(busy, jnp.logical_not(even)))(lambda: step(y_odd, y_even))
    pl.when(jnp.logical_and(jnp.logical_not(busy), even))(lambda: epilogue_slice(y_odd))
    pl.when(jnp.logical_and(jnp.logical_not(busy), jnp.logical_not(even)))(lambda: epilogue_slice(y_even))


def matmul_residual(a, w, bias, x, nw_post, gate, nw_next, sc_next, sh_next, emit_next,
                    tm_prefs=(512, 256, 128, 64), tn_prefs=(512, 256, 128)):
    m, kdim = a.shape
    d = w.shape[1]
    tm, tn = _pick(m, tm_prefs), _pick(d, tn_prefs)
    nn = d // tn
    n_tiles = m // tm
    assert (tm // nn) % EPI_ROWS == 0
    row = pl.BlockSpec((1, d), lambda i, j: (0, 0))
    tile = pl.BlockSpec((tm, d), lambda i, j: (jnp.maximum(i - 1, 0), 0))
    out_shape = [jax.ShapeDtypeStruct((m, d), F32)]
    out_specs = [tile]
    if emit_next:
        out_shape.append(jax.ShapeDtypeStruct((m, d), BF16))
        out_specs.append(tile)
    res = pl.pallas_call(
        functools.partial(_mm_res_kernel, nn=nn),
        grid=(n_tiles + 1, nn),
        in_specs=[pl.BlockSpec((tm, kdim), lambda i, j: (jnp.minimum(i, n_tiles - 1), 0)),
                  pl.BlockSpec((kdim, tn), lambda i, j: (0, j)),
                  row, tile, row, row, row, row, row],
        out_specs=out_specs,
        out_shape=out_shape,
        scratch_shapes=[pltpu.VMEM((nn, tm, tn), F32), pltpu.VMEM((nn, tm, tn), F32)],
        compiler_params=_cparams(("arbitrary", "arbitrary")),
        name="matmul_residual",
    )(a, w, bias, x, nw_post, gate, nw_next, sc_next, sh_next)
    return (res[0], res[1]) if emit_next else (res[0], None)


LANES = 128


def _halo_specs(tm, tc, t):
    per = tm // HALO
    last = t // HALO - 1
    return [pl.BlockSpec((HALO, tc), lambda i, j: (jnp.maximum(i * per - 1, 0), j)),
            pl.BlockSpec((tm, tc), lambda i, j: (i, j)),
            pl.BlockSpec((HALO, tc), lambda i, j: (jnp.minimum((i + 1) * per, last), j))]


def _fill_ext(up_ref, uc_ref, un_ref, ext_ref):
    i, n = pl.program_id(0), pl.num_programs(0)
    tm = uc_ref.shape[0]
    for s in range(ext_ref.shape[0]):
        cols = slice(s * LANES, (s + 1) * LANES)
        ext_ref[s, 0:HALO, :] = jnp.where(i > 0, up_ref[:, cols].astype(F32), 0.0)
        ext_ref[s, HALO:HALO + tm, :] = uc_ref[:, cols].astype(F32)
        ext_ref[s, HALO + tm:HALO + tm + HALO, :] = jnp.where(i < n - 1, un_ref[:, cols].astype(F32), 0.0)


def _conv_rows(ext_ref, w_ref, slab, base, rows, width):
    pad = width // 2
    cols = slice(slab * LANES, (slab + 1) * LANES)
    acc = None
    for j in range(width):
        term = ext_ref[slab, pl.ds(base + (HALO - pad + j), rows), :] * w_ref[j:j + 1, cols]
        acc = term if acc is None else acc + term
    return acc


CONF_ROWS = 64
LN_ROWS = 16


def _conf_conv_kernel(up_ref, uc_ref, un_ref, w_ref, b_ref, lnw_ref, lnb_ref, o_ref, ext_ref, cv_ref, *, width):
    _fill_ext(up_ref, uc_ref, un_ref, ext_ref)
    n_slabs = ext_ref.shape[0]
    d = n_slabs * LANES

    def step(r, carry):
        base = pl.multiple_of(r * CONF_ROWS, CONF_ROWS)
        for s in range(n_slabs):
            cols = slice(s * LANES, (s + 1) * LANES)
            cv_ref[:, cols] = _conv_rows(ext_ref, w_ref, s, base, CONF_ROWS, width) + b_ref[:, cols]
        for r0 in range(0, CONF_ROWS, LN_ROWS):
            u = cv_ref[r0:r0 + LN_ROWS, :]
            xc = u - jnp.mean(u, axis=-1, keepdims=True)
            y = xc * lax.rsqrt(jnp.mean(xc * xc, axis=-1, keepdims=True) + LN_EPS)
            y = y * lnw_ref[...] + lnb_ref[...]
            o_ref[pl.ds(base + r0, LN_ROWS), :] = _silu(y).astype(o_ref.dtype)
        return carry

    lax.fori_loop(0, uc_ref.shape[0] // CONF_ROWS, step, 0)


def conformer_conv_ln(u, dw_w, dw_b, ln_w, ln_b):
    t, d = u.shape
    width = dw_w.shape[0]
    assert width // 2 <= HALO and d % LANES == 0
    tm = _pick(t, (256, 128, 64))
    row = pl.BlockSpec((1, d), lambda i, j: (0, 0))
    return pl.pallas_call(
        functools.partial(_conf_conv_kernel, width=width),
        grid=(t // tm, 1),
        in_specs=_halo_specs(tm, d, t) + [pl.BlockSpec((width, d), lambda i, j: (0, 0)), row, row, row],
        out_specs=pl.BlockSpec((tm, d), lambda i, j: (i, 0)),
        out_shape=jax.ShapeDtypeStruct((t, d), BF16),
        scratch_shapes=[pltpu.VMEM((d // LANES, tm + 2 * HALO, LANES), F32), pltpu.VMEM((CONF_ROWS, d), F32)],
        compiler_params=_cparams(("arbitrary", "arbitrary")),
        name="conformer_conv_ln",
    )(u, u, u, dw_w, dw_b, ln_w, ln_b)


QKV_ROWS = 64


def _qkv_conv_kernel(up_ref, uc_ref, un_ref, w_ref, o_ref, ext_ref, *, width, q_tiles, qk_tiles, q_scale):
    _fill_ext(up_ref, uc_ref, un_ref, ext_ref)
    j = pl.program_id(1)
    scale = jnp.where(j < q_tiles, q_scale, 1.0).astype(F32)

    def make_step(normalise):
        def step(r, carry):
            base = pl.multiple_of(r * QKV_ROWS, QKV_ROWS)
            for s in range(ext_ref.shape[0]):
                y = _silu(_conv_rows(ext_ref, w_ref, s, base, QKV_ROWS, width))
                if normalise:
                    y = y * (lax.rsqrt(jnp.sum(y * y, axis=-1, keepdims=True) + L2_EPS) * scale)
                o_ref[pl.ds(base, QKV_ROWS), s * LANES:(s + 1) * LANES] = y.astype(o_ref.dtype)
            return carry
        return step

    nsteps = uc_ref.shape[0] // QKV_ROWS

    @pl.when(j < qk_tiles)
    def _():
        lax.fori_loop(0, nsteps, make_step(True), 0)

    @pl.when(j >= qk_tiles)
    def _():
        lax.fori_loop(0, nsteps, make_step(False), 0)


def qkv_conv(proj, conv_w, qk_dim):
    t = proj.shape[0]
    width, cdim = conv_w.shape
    assert HEAD_DIM == LANES and width // 2 <= HALO
    tm = _pick(t, (512, 256, 128, 64, 32))
    tc = _pick(qk_dim, (1024, 512, 256, 128))
    return pl.pallas_call(
        functools.partial(_qkv_conv_kernel, width=width, q_tiles=qk_dim // tc,
                          qk_tiles=2 * qk_dim // tc, q_scale=HEAD_DIM ** -0.5),
        grid=(t // tm, cdim // tc),
        in_specs=_halo_specs(tm, tc, t) + [pl.BlockSpec((width, tc), lambda i, j: (0, j))],
        out_specs=pl.BlockSpec((tm, tc), lambda i, j: (i, j)),
        out_shape=jax.ShapeDtypeStruct((t, cdim), BF16),
        scratch_shapes=[pltpu.VMEM((tc // LANES, tm + 2 * HALO, LANES), F32)],
        compiler_params=_cparams(("arbitrary", "arbitrary")),
        name="qkv_conv",
    )(proj, proj, proj, conv_w)


def _split3(x):
    hi = x.astype(BF16)
    r = x - hi.astype(F32)
    mid = r.astype(BF16)
    lo = (r - mid.astype(F32)).astype(BF16)
    return hi, mid, lo


def _gate_kernel(ba_ref, alog_ref, dt_ref, o_ref, *, n_heads):
    tm = ba_ref.shape[0]
    ri = lax.broadcasted_iota(jnp.int32, (CHUNK, CHUNK), 0)
    ci = lax.broadcasted_iota(jnp.int32, (CHUNK, CHUNK), 1)
    tril = jnp.where(ri >= ci, 1.0, 0.0).astype(BF16)
    triu = jnp.where(ri <= ci, 1.0, 0.0).astype(BF16)
    lane = lax.broadcasted_iota(jnp.int32, (CHUNK, 4 * n_heads), 1)
    for c in range(tm // CHUNK):
        ba = ba_ref[c * CHUNK:(c + 1) * CHUNK, :]
        beta = _sigmoid(ba)
        x = ba + dt_ref[...]
        softplus = jnp.maximum(x, 0.0) + jnp.log(1.0 + jnp.exp(-jnp.abs(x)))
        g = -jnp.exp(alog_ref[...]) * softplus
        pre = jnp.zeros_like(g)
        suf = jnp.zeros_like(g)
        for part in _split3(g):
            pre = pre + jnp.dot(tril, part, preferred_element_type=F32)
            suf = suf + jnp.dot(triu, part, preferred_element_type=F32)
        cum = jnp.where(lane < 3 * n_heads, pre, suf)
        o_ref[c * CHUNK:(c + 1) * CHUNK, :] = jnp.where(lane < 2 * n_heads, beta, cum)


def gdn_gates(ba, a_log, dt_bias):
    t, n = ba.shape
    nv = n // 4
    tm = _pick(t, (512, 256, 128, 64))
    zeros = jnp.zeros((2 * nv,), F32)
    alog_row = jnp.concatenate([zeros, a_log.reshape(-1).astype(F32)]).reshape(1, n)
    dt_row = jnp.concatenate([zeros, dt_bias.reshape(-1).astype(F32)]).reshape(1, n)
    row = pl.BlockSpec((1, n), lambda i: (0, 0))
    return pl.pallas_call(
        functools.partial(_gate_kernel, n_heads=nv),
        grid=(t // tm,),
        in_specs=[pl.BlockSpec((tm, n), lambda i: (i, 0)), row, row],
        out_specs=pl.BlockSpec((tm, n), lambda i: (i, 0)),
        out_shape=jax.ShapeDtypeStruct((t, n), F32),
        compiler_params=_cparams(("arbitrary",)),
        name="gdn_gates",
    )(ba, alog_row, dt_row)


def _bdot(a, b, ca=2, cb=1):
    return lax.dot_general(a.astype(BF16), b.astype(BF16), (((ca,), (cb,)), ((0,), (0,))),
                           preferred_element_type=F32)


N_STREAMS = 4


class _PackedMasks:
    def __init__(self):
        c, w = CHUNK, N_STREAMS * CHUNK
        ri = lax.broadcasted_iota(jnp.int32, (c, w), 0)
        li = lax.broadcasted_iota(jnp.int32, (c, w), 1)
        ci = jnp.bitwise_and(li, c - 1)
        fwd = li < (N_STREAMS // 2) * c
        self.eye = ri == ci
        bwd = jnp.logical_not(fwd)
        self.incl = jnp.logical_or(jnp.logical_and(fwd, ri >= ci), jnp.logical_and(bwd, ri <= ci))
        self.strict = jnp.logical_or(jnp.logical_and(fwd, ri > ci), jnp.logical_and(bwd, ri < ci))
        self.seg = [jnp.right_shift(li, 6) == s for s in range(N_STREAMS)]
        self.same16 = jnp.right_shift(ri, 4) == jnp.right_shift(ci, 4)
        self.same32 = jnp.right_shift(ri, 5) == jnp.right_shift(ci, 5)
        rb = lax.broadcasted_iota(jnp.int32, (w, w), 0)
        lb = lax.broadcasted_iota(jnp.int32, (w, w), 1)
        self.blockdiag = jnp.right_shift(rb, 6) == jnp.right_shift(lb, 6)


    def spread(self, cols):
        out = cols[-1]
        for s in range(N_STREAMS - 2, -1, -1):
            out = jnp.where(self.seg[s], cols[s], out)
        return out

    def columns(self, row):
        z = jnp.where(self.eye, row, 0.0)
        return [jnp.sum(jnp.where(self.seg[s], z, 0.0), axis=-1, keepdims=True) for s in range(N_STREAMS)]

    def bd(self, xp):
        return jnp.where(self.blockdiag, jnp.concatenate([xp] * N_STREAMS, axis=-2), 0.0).astype(BF16)


def _inv_unit_triangular_packed(lp, m):
    mm = lambda xp, yp: _bdot(xp, m.bd(yp))
    l16 = jnp.where(m.same16, lp, 0.0)
    c32 = jnp.where(jnp.logical_and(m.same32, jnp.logical_not(m.same16)), lp, 0.0)
    c64 = jnp.where(m.same32, 0.0, lp)
    p = jnp.where(m.eye, 1.0, 0.0) - l16
    sq = mm(l16, l16)
    for _ in range(2):
        both = mm(jnp.concatenate([p, sq], axis=-2), sq)
        p, sq = p + both[:, :CHUNK], both[:, CHUNK:]
    p = p + mm(p, sq)
    y = p - mm(mm(p, c32), p)
    return y - mm(mm(y, c64), y)


def _gdn_prep_kernel(q_ref, k_ref, v_ref, tab_ref, gl_ref, pr_ref, qo_ref, *, chunks):
    c, dk, g = CHUNK, HEAD_DIM, PREP_GROUP
    m = _PackedMasks()

    def group_step(i, carry):
        j0 = pl.multiple_of(i * g, g)
        rows = pl.ds(pl.multiple_of(i * (g * c), g * c), g * c)
        q = q_ref[rows, :].reshape(g, c, dk)
        k = k_ref[rows, :].reshape(g, c, dk)
        v = v_ref[rows, :].reshape(g, c, 2 * dk)
        tab, gl = tab_ref[pl.ds(j0, g)], gl_ref[pl.ds(j0, g)]
        beta_p, g_p = tab[:, 0:1, :], tab[:, 1:2, :]
        x = _bdot(jnp.concatenate([q, k], axis=1), jnp.concatenate([k, k], axis=1), 2, 2)
        qk_p = jnp.concatenate([x[:, :c], x[:, :c]], axis=2)
        kk_p = jnp.concatenate([x[:, c:], x[:, c:]], axis=2)
        g_cols = m.columns(g_p)
        g_cb = m.spread(g_cols)
        beta_cb = m.spread(m.columns(beta_p))
        decay = jnp.exp(jnp.where(m.incl, g_cb - g_p, -jnp.inf))
        l_p = jnp.where(m.strict, kk_p * decay * beta_cb, 0.0)
        a_p = qk_p * decay
        t_p = _inv_unit_triangular_packed(l_p, m)
        k_st = jnp.concatenate([k] * N_STREAMS, axis=1)
        v_st = jnp.concatenate([v[:, :, :dk], v[:, :, dk:]] * (N_STREAMS // 2), axis=1)
        g_cst = jnp.concatenate(g_cols, axis=1)
        eg_cst = jnp.exp(g_cst)
        vk = jnp.concatenate([v_st, (k_st.astype(F32) * eg_cst).astype(BF16)], axis=2)
        uw = _bdot(m.bd(t_p * beta_p), vk).astype(BF16)
        a_uw = _bdot(m.bd(a_p), uw)
        q_st = jnp.concatenate([q] * N_STREAMS, axis=1).astype(F32)
        q_prime = q_st * eg_cst - a_uw[:, :, dk:]
        qo = jnp.concatenate([q_prime, a_uw[:, :, :dk]], axis=2).astype(qo_ref.dtype)
        gl_st = jnp.concatenate([jnp.broadcast_to(gl[:, s:s + 1, :], (g, c, dk)) for s in range(N_STREAMS)], axis=1)
        kt_st = (k_st.astype(F32) * jnp.exp(gl_st - g_cst)).astype(BF16)
        for s in range(N_STREAMS):
            seg = slice(s * c, (s + 1) * c)
            d, vh = divmod(s, 2)
            pr_ref[d, vh, pl.ds(j0, g)] = _bdot(kt_st[:, seg], uw[:, seg], 1, 1).astype(pr_ref.dtype)
            qo_ref[d, vh, rows, :] = qo[:, seg].reshape(g * c, 2 * dk)
        return carry

    lax.fori_loop(0, chunks // g, group_step, 0)


SCAN_HEADS = 8


def _gdn_scan_kernel(prf_ref, prb_ref, qof_ref, qob_ref, glf_ref, glb_ref, of_ref, ob_ref, s_ref, *, chunks):
    c, dk = CHUNK, HEAD_DIM

    @pl.when(pl.program_id(1) == 0)
    def _():
        s_ref[...] = jnp.zeros_like(s_ref)

    def chunk_step(j, carry):
        for d, (pr_ref, qo_ref, gl_ref, o_ref) in enumerate(((prf_ref, qof_ref, glf_ref, of_ref),
                                                             (prb_ref, qob_ref, glb_ref, ob_ref))):
            cj = j if d == 0 else chunks - 1 - j
            rows = pl.ds(pl.multiple_of(cj * c, c), c)
            for pair in range(SCAN_HEADS // 2):
                vhs = (2 * pair, 2 * pair + 1)
                prs = [pr_ref[vh, cj] for vh in vhs]
                qos = [qo_ref[vh, rows, :] for vh in vhs]
                ss = [s_ref[d * SCAN_HEADS + vh] for vh in vhs]
                lhs = jnp.concatenate([jnp.concatenate([pr[:, dk:], qo[:, :dk]], axis=0)
                                       for pr, qo in zip(prs, qos)], axis=1)
                zero = jnp.zeros((dk, dk), BF16)
                s_bd = jnp.concatenate([jnp.concatenate([ss[0].astype(BF16), zero], axis=1),
                                        jnp.concatenate([zero, ss[1].astype(BF16)], axis=1)], axis=0)
                x = jnp.dot(lhs, s_bd, preferred_element_type=F32)
                for n, vh in enumerate(vhs):
                    xs = x[:, n * dk:(n + 1) * dk]
                    o_ref[rows, vh * dk:(vh + 1) * dk] = (xs[dk:] + qos[n][:, dk:].astype(F32)).astype(o_ref.dtype)
                    s_ref[d * SCAN_HEADS + vh] = (ss[n] * jnp.exp(gl_ref[vh, cj])
                                                  + prs[n][:, :dk].astype(F32) - xs[:dk])
        return carry

    lax.fori_loop(0, chunks, chunk_step, 0)


def _gated_norm_kernel(of_ref, ob_ref, z_ref, nw_ref, y_ref):
    o = of_ref[...].astype(F32) + ob_ref[...].astype(F32)
    gate = _silu(z_ref[...].astype(F32))
    for h in range(o.shape[1] // HEAD_DIM):
        cols = slice(h * HEAD_DIM, (h + 1) * HEAD_DIM)
        y_ref[:, cols] = (_rms(o[:, cols]) * nw_ref[...] * gate[:, cols]).astype(y_ref.dtype)


PREP_GROUP = 8
PREP_CHUNKS = 16
SCAN_CHUNKS = 8


def gdn_core(qkv, proj, gates, norm_w, n_qk, n_v):
    t = qkv.shape[0]
    c, dk = CHUNK, HEAD_DIM
    dv = dk
    assert n_v == 2 * n_qk and t % c == 0 and n_v % SCAN_HEADS == 0
    nc = t // c
    qk_dim, v_dim = n_qk * dk, n_v * dv
    beta = gates[:, :2 * n_v].reshape(nc, c, 2, n_qk, 2)
    cum = gates[:, 2 * n_v:].reshape(nc, c, 2, n_qk, 2)
    packed = lambda a: a.transpose(3, 0, 2, 4, 1).reshape(n_qk, nc, 1, N_STREAMS * c)
    tab = jnp.concatenate([packed(beta), packed(cum), jnp.zeros((n_qk, nc, 6, N_STREAMS * c), F32)], axis=2)
    last = jnp.stack([cum[:, c - 1, 0], cum[:, 0, 1]])
    gl_prep = last.transpose(2, 1, 0, 3).reshape(n_qk, nc, N_STREAMS, 1)
    gl_prep = jnp.broadcast_to(jnp.pad(gl_prep, ((0, 0), (0, 0), (0, 8 - N_STREAMS), (0, 0))), (n_qk, nc, 8, dk))
    gl_scan = jnp.broadcast_to(last.reshape(2, nc, n_v).transpose(0, 2, 1)[..., None, None], (2, n_v, nc, 1, dv))

    pc = PREP_CHUNKS
    assert nc % pc == 0 and pc % PREP_GROUP == 0
    rb = pc * c
    pr, qo = pl.pallas_call(
        functools.partial(_gdn_prep_kernel, chunks=pc),
        grid=(n_qk, nc // pc),
        in_specs=[pl.BlockSpec((rb, dk), lambda h, b: (b, h)),
                  pl.BlockSpec((rb, dk), lambda h, b: (b, n_qk + h)),
                  pl.BlockSpec((rb, 2 * dv), lambda h, b: (b, qk_dim // dv + h)),
                  pl.BlockSpec((None, pc, 8, N_STREAMS * c), lambda h, b: (h, b, 0, 0)),
                  pl.BlockSpec((None, pc, 8, dk), lambda h, b: (h, b, 0, 0))],
        out_specs=[pl.BlockSpec((2, 2, pc, dk, 2 * dv), lambda h, b: (0, h, b, 0, 0)),
                   pl.BlockSpec((2, 2, rb, 2 * dv), lambda h, b: (0, h, b, 0))],
        out_shape=[jax.ShapeDtypeStruct((2, n_v, nc, dk, 2 * dv), BF16),
                   jax.ShapeDtypeStruct((2, n_v, t, 2 * dv), BF16)],
        compiler_params=_cparams(("arbitrary", "arbitrary")),
        name="gdn_prep",
    )(qkv, qkv, qkv, tab, gl_prep)

    sc = _pick(nc, (SCAN_CHUNKS, 4, 2, 1))
    ng = nc // sc
    rs = sc * c
    fwd = lambda *tail: (lambda g, b: (0, g, b) + tail)
    bwd = lambda *tail: (lambda g, b: (1, g, ng - 1 - b) + tail)
    o_f, o_b = pl.pallas_call(
        functools.partial(_gdn_scan_kernel, chunks=sc),
        grid=(n_v // SCAN_HEADS, ng),
        in_specs=[pl.BlockSpec((None, SCAN_HEADS, sc, dk, 2 * dv), fwd(0, 0)),
                  pl.BlockSpec((None, SCAN_HEADS, sc, dk, 2 * dv), bwd(0, 0)),
                  pl.BlockSpec((None, SCAN_HEADS, rs, 2 * dv), fwd(0)),
                  pl.BlockSpec((None, SCAN_HEADS, rs, 2 * dv), bwd(0)),
                  pl.BlockSpec((None, SCAN_HEADS, sc, 1, dv), fwd(0, 0)),
                  pl.BlockSpec((None, SCAN_HEADS, sc, 1, dv), bwd(0, 0))],
        out_specs=[pl.BlockSpec((rs, SCAN_HEADS * dv), lambda g, b: (b, g)),
                   pl.BlockSpec((rs, SCAN_HEADS * dv), lambda g, b: (ng - 1 - b, g))],
        out_shape=[jax.ShapeDtypeStruct((t, v_dim), BF16)] * 2,
        scratch_shapes=[pltpu.VMEM((2 * SCAN_HEADS, dk, dv), F32)],
        compiler_params=_cparams(("arbitrary", "arbitrary")),
        name="gdn_scan",
    )(pr, pr, qo, qo, gl_scan, gl_scan)

    tm = _pick(t, (512, 256, 128, 64))
    tc = _pick(v_dim, (1024, 512, 256, 128))
    z_off = (2 * qk_dim + v_dim) // tc
    tile = pl.BlockSpec((tm, tc), lambda i, j: (i, j))
    return pl.pallas_call(
        _gated_norm_kernel,
        grid=(t // tm, v_dim // tc),
        in_specs=[tile, tile, pl.BlockSpec((tm, tc), lambda i, j: (i, z_off + j)),
                  pl.BlockSpec((1, dv), lambda i, j: (0, 0))],
        out_specs=tile,
        out_shape=jax.ShapeDtypeStruct((t, v_dim), BF16),
        compiler_params=_cparams(("arbitrary", "arbitrary")),
        name="gdn_gated_norm",
    )(o_f, o_b, proj, norm_w.reshape(1, dv).astype(F32))


def kernel(x, c, ada_w, ada_b, norm_w, gdn_in_w, gdn_conv_w, gdn_A_log, gdn_dt_bias, gdn_norm_w, gdn_out_w, cf_pw1_w, cf_pw1_b, cf_dw_w, cf_dw_b, cf_ln_w, cf_ln_b, cf_pw2_w, cf_pw2_b, ffn_in_w, ffn_out_w):
    bsz, t, d = x.shape
    assert bsz == 1
    depth = ada_w.shape[0]
    n_v = gdn_A_log.shape[-1]
    n_qk = n_v // 2
    qk_dim, v_dim = n_qk * HEAD_DIM, n_v * HEAD_DIM
    conv_dim = 2 * qk_dim + v_dim
    row = lambda a: a.reshape(1, -1).astype(F32)
    zero_row = jnp.zeros((1, d), F32)

    mod = ada_modulation(c, ada_w, ada_b)[:, 0].reshape(depth, 6, d)
    xs = x[0]
    h = prenorm(xs, row(norm_w[0, 0]), row(mod[0, 1]), row(mod[0, 0]))
    for i in range(depth):
        j = i // N_MIXERS
        if i % N_MIXERS == 0:
            proj = matmul(h, gdn_in_w, j, 0, conv_dim + v_dim, BF16)
            ba = matmul(h, gdn_in_w, j, conv_dim + v_dim, 4 * n_v, F32)
            gates = gdn_gates(ba, gdn_A_log[j], gdn_dt_bias[j])
            qkv = qkv_conv(proj, gdn_conv_w[j].astype(F32), qk_dim)
            mix = gdn_core(qkv, proj, gates, gdn_norm_w[j], n_qk, n_v)
            w_out, b_out = gdn_out_w[j].astype(BF16), zero_row
        else:
            u = matmul_glu(h, cf_pw1_w, j, row(cf_pw1_b[j]), "glu")
            mix = conformer_conv_ln(u, cf_dw_w[j].astype(F32), row(cf_dw_b[j]), row(cf_ln_w[j]), row(cf_ln_b[j]))
            w_out, b_out = cf_pw2_w[j].astype(BF16), row(cf_pw2_b[j])
        xs, h = matmul_residual(mix, w_out, b_out, xs, row(norm_w[i, 1]), row(mod[i, 2]),
                                row(norm_w[i, 2]), row(mod[i, 4]), row(mod[i, 3]), True)
        f = matmul_glu(h, ffn_in_w, i, jnp.zeros((1, ffn_in_w.shape[2]), F32), "swiglu")
        last = i == depth - 1
        nxt = (zero_row, zero_row, zero_row) if last else (
            row(norm_w[i + 1, 0]), row(mod[i + 1, 1]), row(mod[i + 1, 0]))
        xs, h = matmul_residual(f, ffn_out_w[i].astype(BF16), zero_row, xs, row(norm_w[i, 3]),
                                row(mod[i, 5]), *nxt, not last)
    return xs[None]
```

```python
import functools

import jax
import jax.numpy as jnp
from jax import lax
from jax.experimental import pallas as pl
from jax.experimental.pallas import tpu as pltpu

F32 = jnp.float32
BF16 = jnp.bfloat16

HEAD_DIM = 128
N_MIXERS = 2
CHUNK = 64
RMS_EPS = 1e-6
LN_EPS = 1e-5
L2_EPS = 1e-6

V7X_VMEM_BYTES = 64 * 1024 * 1024
VMEM_LIMIT = V7X_VMEM_BYTES - 12 * 1024 * 1024
HALO = 16


def _cparams(sem):
    return pltpu.CompilerParams(dimension_semantics=sem, vmem_limit_bytes=VMEM_LIMIT)


def _pick(n, prefs):
    for p in prefs:
        if n % p == 0:
            return p
    return n


def _sigmoid(x):
    return 1.0 / (1.0 + jnp.exp(-x))


def _silu(x):
    return x * _sigmoid(x)


def _rms(y):
    return y * lax.rsqrt(jnp.mean(y * y, axis=-1, keepdims=True) + RMS_EPS)


def _ada_kernel(c_ref, w_ref, b_ref, o_ref):
    cond = _silu(c_ref[...])
    o_ref[...] = jnp.dot(cond.astype(BF16), w_ref[...].astype(BF16),
                         preferred_element_type=F32) + b_ref[...]


def ada_modulation(c, ada_w, ada_b):
    depth, d, n = ada_w.shape
    b = c.shape[0]
    assert b <= 8
    cp = jnp.zeros((8, d), F32).at[:b].set(c)
    tn = _pick(n, (1024, 512, 256, 128))
    out = pl.pallas_call(
        _ada_kernel,
        grid=(depth, n // tn),
        in_specs=[pl.BlockSpec((8, d), lambda l, j: (0, 0)),
                  pl.BlockSpec((None, d, tn), lambda l, j: (l, 0, j)),
                  pl.BlockSpec((None, 1, tn), lambda l, j: (l, 0, j))],
        out_specs=pl.BlockSpec((None, 8, tn), lambda l, j: (l, 0, j)),
        out_shape=jax.ShapeDtypeStruct((depth, 8, n), F32),
        compiler_params=_cparams(("arbitrary", "arbitrary")),
        name="ada_modulation",
    )(cp, ada_w, ada_b.reshape(depth, 1, n))
    return out[:, :b]


def _prenorm_kernel(x_ref, nw_ref, sc_ref, sh_ref, h_ref):
    y = _rms(x_ref[...])
    h_ref[...] = (y * nw_ref[...] * (1.0 + sc_ref[...]) + sh_ref[...]).astype(h_ref.dtype)


def prenorm(x, nw, sc, sh):
    t, d = x.shape
    tm = _pick(t, (512, 256, 128, 64, 8))
    row = pl.BlockSpec((1, d), lambda i: (0, 0))
    return pl.pallas_call(
        _prenorm_kernel,
        grid=(t // tm,),
        in_specs=[pl.BlockSpec((tm, d), lambda i: (i, 0)), row, row, row],
        out_specs=pl.BlockSpec((tm, d), lambda i: (i, 0)),
        out_shape=jax.ShapeDtypeStruct((t, d), BF16),
        compiler_params=_cparams(("arbitrary",)),
        name="prenorm",
    )(x, nw, sc, sh)


def _mm_kernel(x_ref, w_ref, o_ref, wb_ref):
    @pl.when(pl.program_id(1) == 0)
    def _():
        wb_ref[...] = w_ref[...].astype(BF16)

    o_ref[...] = jnp.dot(x_ref[...], wb_ref[...], preferred_element_type=F32).astype(o_ref.dtype)


def matmul(x, w, layer, col0, n, out_dtype, tm_prefs=(1024, 512, 256, 128, 64), tn_prefs=(1024, 512, 256, 128)):
    m, k = x.shape
    tm, tn = _pick(m, tm_prefs), _pick(n, tn_prefs)
    assert col0 % tn == 0
    j0 = col0 // tn
    return pl.pallas_call(
        _mm_kernel,
        grid=(n // tn, m // tm),
        in_specs=[pl.BlockSpec((tm, k), lambda j, i: (i, 0)),
                  pl.BlockSpec((None, k, tn), lambda j, i: (layer, 0, j0 + j))],
        out_specs=pl.BlockSpec((tm, tn), lambda j, i: (i, j)),
        out_shape=jax.ShapeDtypeStruct((m, n), out_dtype),
        scratch_shapes=[pltpu.VMEM((k, tn), BF16)],
        compiler_params=_cparams(("arbitrary", "arbitrary")),
        name="matmul",
    )(x, w)


def _mm_glu_kernel(x_ref, wa_ref, wb_ref, ba_ref, bb_ref, o_ref, wab_ref, wbb_ref, *, act):
    @pl.when(pl.program_id(1) == 0)
    def _():
        wab_ref[...] = wa_ref[...].astype(BF16)
        wbb_ref[...] = wb_ref[...].astype(BF16)

    x = x_ref[...]
    a = jnp.dot(x, wab_ref[...], preferred_element_type=F32) + ba_ref[...]
    b = jnp.dot(x, wbb_ref[...], preferred_element_type=F32) + bb_ref[...]
    if act == "swiglu":
        r = _silu(a) * b
    else:
        r = a * _sigmoid(b)
    o_ref[...] = r.astype(o_ref.dtype)


def matmul_glu(x, w, layer, bias, act, tm_prefs=(1024, 512, 256, 128, 64), tn_prefs=(512, 256, 128)):
    m, k = x.shape
    n = w.shape[2] // 2
    tm, tn = _pick(m, tm_prefs), _pick(n, tn_prefs)
    nb = n // tn
    return pl.pallas_call(
        functools.partial(_mm_glu_kernel, act=act),
        grid=(nb, m // tm),
        in_specs=[pl.BlockSpec((tm, k), lambda j, i: (i, 0)),
                  pl.BlockSpec((None, k, tn), lambda j, i: (layer, 0, j)),
                  pl.BlockSpec((None, k, tn), lambda j, i: (layer, 0, j + nb)),
                  pl.BlockSpec((1, tn), lambda j, i: (0, j)),
                  pl.BlockSpec((1, tn), lambda j, i: (0, j + nb))],
        out_specs=pl.BlockSpec((tm, tn), lambda j, i: (i, j)),
        out_shape=jax.ShapeDtypeStruct((m, n), BF16),
        scratch_shapes=[pltpu.VMEM((k, tn), BF16), pltpu.VMEM((k, tn), BF16)],
        compiler_params=_cparams(("arbitrary", "arbitrary")),
        name="matmul_" + act,
    )(x, w, w, bias, bias)


EPI_ROWS = 32


def _mm_res_kernel(a_ref, w_ref, b_ref, x_ref, nwp_ref, g_ref, nwn_ref, sc_ref, sh_ref,
                   xo_ref, *rest, nn):
    *maybe_ho_ref, y_even, y_odd = rest
    i, j = pl.program_id(0), pl.program_id(1)
    n_tiles = pl.num_programs(0) - 1
    tm, tn = y_even.shape[1], y_even.shape[2]
    d = nn * tn
    rows_per_step = tm // nn
    col = lambda ref, jj: ref[:, jj * tn:(jj + 1) * tn]

    @pl.when(jnp.logical_and(i == 0, j == 0))
    def _():
        y_odd[...] = jnp.zeros_like(y_odd)

    def epilogue_slice(y_prev):
        for r in range(rows_per_step // EPI_ROWS):
            rows = pl.ds(r * EPI_ROWS, EPI_ROWS)
            yrows = pl.ds(pl.multiple_of(j * rows_per_step + r * EPI_ROWS, EPI_ROWS), EPI_ROWS)
            ys = [y_prev[jj, yrows, :] + col(b_ref, jj) for jj in range(nn)]
            inv = lax.rsqrt(sum(jnp.sum(y * y, axis=-1, keepdims=True) for y in ys) / d + RMS_EPS)
            xns = [x_ref[rows, jj * tn:(jj + 1) * tn] + (1.0 + col(g_ref, jj)) * (ys[jj] * inv * col(nwp_ref, jj))
                   for jj in range(nn)]
            for jj in range(nn):
                xo_ref[rows, jj * tn:(jj + 1) * tn] = xns[jj]
            if maybe_ho_ref:
                inv2 = lax.rsqrt(sum(jnp.sum(v * v, axis=-1, keepdims=True) for v in xns) / d + RMS_EPS)
                for jj in range(nn):
                    hn = xns[jj] * inv2 * col(nwn_ref, jj) * (1.0 + col(sc_ref, jj)) + col(sh_ref, jj)
                    maybe_ho_ref[0][rows, jj * tn:(jj + 1) * tn] = hn.astype(BF16)

    def step(y_cur, y_prev):
        y_cur[j] = jnp.dot(a_ref[...], w_ref[...], preferred_element_type=F32)
        epilogue_slice(y_prev)

    even, busy = lax.rem(i, 2) == 0, i < n_tiles
    pl.when(jnp.logical_and(busy, even))(lambda: step(y_even, y_odd))
    pl.when(jnp.logical_and(busy, jnp.logical_not(even)))(lambda: step(y_odd, y_even))
    pl.when(jnp.logical_and(jnp.logical_not(busy), even))(lambda: epilogue_slice(y_odd))
    pl.when(jnp.logical_and(jnp.logical_not(busy), jnp.logical_not(even)))(lambda: epilogue_slice(y_even))


def matmul_residual(a, w, bias, x, nw_post, gate, nw_next, sc_next, sh_next, emit_next,
                    tm_prefs=(512, 256, 128, 64), tn_prefs=(512, 256, 128)):
    m, kdim = a.shape
    d = w.shape[1]
    tm, tn = _pick(m, tm_prefs), _pick(d, tn_prefs)
    nn = d // tn
    n_tiles = m // tm
    assert (tm // nn) % EPI_ROWS == 0
    row = pl.BlockSpec((1, d), lambda i, j: (0, 0))
    tile = pl.BlockSpec((tm // nn, d), lambda i, j: (jnp.maximum((i - 1) * nn + j, 0), 0))
    out_shape = [jax.ShapeDtypeStruct((m, d), F32)]
    out_specs = [tile]
    if emit_next:
        out_shape.append(jax.ShapeDtypeStruct((m, d), BF16))
        out_specs.append(tile)
    res = pl.pallas_call(
        functools.partial(_mm_res_kernel, nn=nn),
        grid=(n_tiles + 1, nn),
        in_specs=[pl.BlockSpec((tm, kdim), lambda i, j: (jnp.minimum(i, n_tiles - 1), 0)),
                  pl.BlockSpec((kdim, tn), lambda i, j: (0, j)),
                  row, tile, row, row, row, row, row],
        out_specs=out_specs,
        out_shape=out_shape,
        scratch_shapes=[pltpu.VMEM((nn, tm, tn), F32), pltpu.VMEM((nn, tm, tn), F32)],
        compiler_params=_cparams(("arbitrary", "arbitrary")),
        name="matmul_residual",
    )(a, w, bias, x, nw_post, gate, nw_next, sc_next, sh_next)
    return (res[0], res[1]) if emit_next else (res[0], None)


LANES = 128


def _halo_specs(tm, tc, t):
    per = tm // HALO
    last = t // HALO - 1
    return [pl.BlockSpec((HALO, tc), lambda i, j: (jnp.maximum(i * per - 1, 0), j)),
            pl.BlockSpec((tm, tc), lambda i, j: (i, j)),
            pl.BlockSpec((HALO, tc), lambda i, j: (jnp.minimum((i + 1) * per, last), j))]


def _fill_ext(up_ref, uc_ref, un_ref, ext_ref):
    i, n = pl.program_id(0), pl.num_programs(0)
    tm = uc_ref.shape[0]
    for s in range(ext_ref.shape[0]):
        cols = slice(s * LANES, (s + 1) * LANES)
        ext_ref[s, 0:HALO, :] = jnp.where(i > 0, up_ref[:, cols].astype(F32), 0.0)
        ext_ref[s, HALO:HALO + tm, :] = uc_ref[:, cols].astype(F32)
        ext_ref[s, HALO + tm:HALO + tm + HALO, :] = jnp.where(i < n - 1, un_ref[:, cols].astype(F32), 0.0)


def _conv_rows(ext_ref, w_ref, slab, base, rows, width):
    pad = width // 2
    cols = slice(slab * LANES, (slab + 1) * LANES)
    acc = None
    for j in range(width):
        term = ext_ref[slab, pl.ds(base + (HALO - pad + j), rows), :] * w_ref[j:j + 1, cols]
        acc = term if acc is None else acc + term
    return acc


CONF_ROWS = 64
LN_ROWS = 16


def _conf_conv_kernel(up_ref, uc_ref, un_ref, w_ref, b_ref, lnw_ref, lnb_ref, o_ref, ext_ref, cv_ref, *, width):
    _fill_ext(up_ref, uc_ref, un_ref, ext_ref)
    n_slabs = ext_ref.shape[0]
    d = n_slabs * LANES

    def step(r, carry):
        base = pl.multiple_of(r * CONF_ROWS, CONF_ROWS)
        for s in range(n_slabs):
            cols = slice(s * LANES, (s + 1) * LANES)
            cv_ref[:, cols] = _conv_rows(ext_ref, w_ref, s, base, CONF_ROWS, width) + b_ref[:, cols]
        for r0 in range(0, CONF_ROWS, LN_ROWS):
            u = cv_ref[r0:r0 + LN_ROWS, :]
            xc = u - jnp.mean(u, axis=-1, keepdims=True)
            y = xc * lax.rsqrt(jnp.mean(xc * xc, axis=-1, keepdims=True) + LN_EPS)
            y = y * lnw_ref[...] + lnb_ref[...]
            o_ref[pl.ds(base + r0, LN_ROWS), :] = _silu(y).astype(o_ref.dtype)
        return carry

    lax.fori_loop(0, uc_ref.shape[0] // CONF_ROWS, step, 0)


def conformer_conv_ln(u, dw_w, dw_b, ln_w, ln_b):
    t, d = u.shape
    width = dw_w.shape[0]
    assert width // 2 <= HALO and d % LANES == 0
    tm = _pick(t, (256, 128, 64))
    row = pl.BlockSpec((1, d), lambda i, j: (0, 0))
    return pl.pallas_call(
        functools.partial(_conf_conv_kernel, width=width),
        grid=(t // tm, 1),
        in_specs=_halo_specs(tm, d, t) + [pl.BlockSpec((width, d), lambda i, j: (0, 0)), row, row, row],
        out_specs=pl.BlockSpec((tm, d), lambda i, j: (i, 0)),
        out_shape=jax.ShapeDtypeStruct((t, d), BF16),
        scratch_shapes=[pltpu.VMEM((d // LANES, tm + 2 * HALO, LANES), F32), pltpu.VMEM((CONF_ROWS, d), F32)],
        compiler_params=_cparams(("arbitrary", "arbitrary")),
        name="conformer_conv_ln",
    )(u, u, u, dw_w, dw_b, ln_w, ln_b)


QKV_ROWS = 64


def _qkv_conv_kernel(up_ref, uc_ref, un_ref, w_ref, o_ref, ext_ref, *, width, q_tiles, qk_tiles, q_scale):
    _fill_ext(up_ref, uc_ref, un_ref, ext_ref)
    j = pl.program_id(1)
    scale = jnp.where(j < q_tiles, q_scale, 1.0).astype(F32)

    def make_step(normalise):
        def step(r, carry):
            base = pl.multiple_of(r * QKV_ROWS, QKV_ROWS)
            for s in range(ext_ref.shape[0]):
                y = _silu(_conv_rows(ext_ref, w_ref, s, base, QKV_ROWS, width))
                if normalise:
                    y = y * (lax.rsqrt(jnp.sum(y * y, axis=-1, keepdims=True) + L2_EPS) * scale)
                o_ref[pl.ds(base, QKV_ROWS), s * LANES:(s + 1) * LANES] = y.astype(o_ref.dtype)
            return carry
        return step

    nsteps = uc_ref.shape[0] // QKV_ROWS

    @pl.when(j < qk_tiles)
    def _():
        lax.fori_loop(0, nsteps, make_step(True), 0)

    @pl.when(j >= qk_tiles)
    def _():
        lax.fori_loop(0, nsteps, make_step(False), 0)


def qkv_conv(proj, conv_w, qk_dim):
    t = proj.shape[0]
    width, cdim = conv_w.shape
    assert HEAD_DIM == LANES and width // 2 <= HALO
    tm = _pick(t, (512, 256, 128, 64, 32))
    tc = _pick(qk_dim, (1024, 512, 256, 128))
    return pl.pallas_call(
        functools.partial(_qkv_conv_kernel, width=width, q_tiles=qk_dim // tc,
                          qk_tiles=2 * qk_dim // tc, q_scale=HEAD_DIM ** -0.5),
        grid=(t // tm, cdim // tc),
        in_specs=_halo_specs(tm, tc, t) + [pl.BlockSpec((width, tc), lambda i, j: (0, j))],
        out_specs=pl.BlockSpec((tm, tc), lambda i, j: (i, j)),
        out_shape=jax.ShapeDtypeStruct((t, cdim), BF16),
        scratch_shapes=[pltpu.VMEM((tc // LANES, tm + 2 * HALO, LANES), F32)],
        compiler_params=_cparams(("arbitrary", "arbitrary")),
        name="qkv_conv",
    )(proj, proj, proj, conv_w)


def _split3(x):
    hi = x.astype(BF16)
    r = x - hi.astype(F32)
    mid = r.astype(BF16)
    lo = (r - mid.astype(F32)).astype(BF16)
    return hi, mid, lo


def _gate_kernel(ba_ref, alog_ref, dt_ref, o_ref, *, n_heads):
    tm = ba_ref.shape[0]
    ri = lax.broadcasted_iota(jnp.int32, (CHUNK, CHUNK), 0)
    ci = lax.broadcasted_iota(jnp.int32, (CHUNK, CHUNK), 1)
    tril = jnp.where(ri >= ci, 1.0, 0.0).astype(BF16)
    triu = jnp.where(ri <= ci, 1.0, 0.0).astype(BF16)
    lane = lax.broadcasted_iota(jnp.int32, (CHUNK, 4 * n_heads), 1)
    for c in range(tm // CHUNK):
        ba = ba_ref[c * CHUNK:(c + 1) * CHUNK, :]
        beta = _sigmoid(ba)
        x = ba + dt_ref[...]
        softplus = jnp.maximum(x, 0.0) + jnp.log(1.0 + jnp.exp(-jnp.abs(x)))
        g = -jnp.exp(alog_ref[...]) * softplus
        pre = jnp.zeros_like(g)
        suf = jnp.zeros_like(g)
        for part in _split3(g):
            pre = pre + jnp.dot(tril, part, preferred_element_type=F32)
            suf = suf + jnp.dot(triu, part, preferred_element_type=F32)
        cum = jnp.where(lane < 3 * n_heads, pre, suf)
        o_ref[c * CHUNK:(c + 1) * CHUNK, :] = jnp.where(lane < 2 * n_heads, beta, cum)


def gdn_gates(ba, a_log, dt_bias):
    t, n = ba.shape
    nv = n // 4
    tm = _pick(t, (512, 256, 128, 64))
    zeros = jnp.zeros((2 * nv,), F32)
    alog_row = jnp.concatenate([zeros, a_log.reshape(-1).astype(F32)]).reshape(1, n)
    dt_row = jnp.concatenate([zeros, dt_bias.reshape(-1).astype(F32)]).reshape(1, n)
    row = pl.BlockSpec((1, n), lambda i: (0, 0))
    return pl.pallas_call(
        functools.partial(_gate_kernel, n_heads=nv),
        grid=(t // tm,),
        in_specs=[pl.BlockSpec((tm, n), lambda i: (i, 0)), row, row],
        out_specs=pl.BlockSpec((tm, n), lambda i: (i, 0)),
        out_shape=jax.ShapeDtypeStruct((t, n), F32),
        compiler_params=_cparams(("arbitrary",)),
        name="gdn_gates",
    )(ba, alog_row, dt_row)


def _bdot(a, b, ca=2, cb=1):
    return lax.dot_general(a.astype(BF16), b.astype(BF16), (((ca,), (cb,)), ((0,), (0,))),
                           preferred_element_type=F32)


N_STREAMS = 4


class _PackedMasks:
    def __init__(self):
        c, w = CHUNK, N_STREAMS * CHUNK
        ri = lax.broadcasted_iota(jnp.int32, (c, w), 0)
        li = lax.broadcasted_iota(jnp.int32, (c, w), 1)
        ci = jnp.bitwise_and(li, c - 1)
        fwd = li < (N_STREAMS // 2) * c
        self.eye = ri == ci
        bwd = jnp.logical_not(fwd)
        self.incl = jnp.logical_or(jnp.logical_and(fwd, ri >= ci), jnp.logical_and(bwd, ri <= ci))
        self.strict = jnp.logical_or(jnp.logical_and(fwd, ri > ci), jnp.logical_and(bwd, ri < ci))
        self.seg = [jnp.right_shift(li, 6) == s for s in range(N_STREAMS)]
        self.same16 = jnp.right_shift(ri, 4) == jnp.right_shift(ci, 4)
        self.same32 = jnp.right_shift(ri, 5) == jnp.right_shift(ci, 5)
        rb = lax.broadcasted_iota(jnp.int32, (w, w), 0)
        lb = lax.broadcasted_iota(jnp.int32, (w, w), 1)
        self.blockdiag = jnp.right_shift(rb, 6) == jnp.right_shift(lb, 6)


    def spread(self, cols):
        out = cols[-1]
        for s in range(N_STREAMS - 2, -1, -1):
            out = jnp.where(self.seg[s], cols[s], out)
        return out

    def columns(self, row):
        z = jnp.where(self.eye, row, 0.0)
        return [jnp.sum(jnp.where(self.seg[s], z, 0.0), axis=-1, keepdims=True) for s in range(N_STREAMS)]

    def bd(self, xp):
        return jnp.where(self.blockdiag, jnp.concatenate([xp] * N_STREAMS, axis=-2), 0.0).astype(BF16)


def _inv_unit_triangular_packed(lp, m):
    mm = lambda xp, yp: _bdot(xp, m.bd(yp))
    l16 = jnp.where(m.same16, lp, 0.0)
    c32 = jnp.where(jnp.logical_and(m.same32, jnp.logical_not(m.same16)), lp, 0.0)
    c64 = jnp.where(m.same32, 0.0, lp)
    p = jnp.where(m.eye, 1.0, 0.0) - l16
    sq = mm(l16, l16)
    for _ in range(2):
        both = mm(jnp.concatenate([p, sq], axis=-2), sq)
        p, sq = p + both[:, :CHUNK], both[:, CHUNK:]
    p = p + mm(p, sq)
    y = p - mm(mm(p, c32), p)
    return y - mm(mm(y, c64), y)


def _gdn_prep_kernel(q_ref, k_ref, v_ref, tab_ref, gl_ref, pr_ref, qo_ref, *, chunks):
    c, dk, g = CHUNK, HEAD_DIM, PREP_GROUP
    m = _PackedMasks()

    def group_step(i, carry):
        j0 = pl.multiple_of(i * g, g)
        rows = pl.ds(pl.multiple_of(i * (g * c), g * c), g * c)
        q = q_ref[rows, :].reshape(g, c, dk)
        k = k_ref[rows, :].reshape(g, c, dk)
        v = v_ref[rows, :].reshape(g, c, 2 * dk)
        tab, gl = tab_ref[pl.ds(j0, g)], gl_ref[pl.ds(j0, g)]
        beta_p, g_p = tab[:, 0:1, :], tab[:, 1:2, :]
        x = _bdot(jnp.concatenate([q, k], axis=1), jnp.concatenate([k, k], axis=1), 2, 2)
        qk_p = jnp.concatenate([x[:, :c], x[:, :c]], axis=2)
        kk_p = jnp.concatenate([x[:, c:], x[:, c:]], axis=2)
        g_cols = m.columns(g_p)
        g_cb = m.spread(g_cols)
        beta_cb = m.spread(m.columns(beta_p))
        decay = jnp.exp(jnp.where(m.incl, g_cb - g_p, -jnp.inf))
        l_p = jnp.where(m.strict, kk_p * decay * beta_cb, 0.0)
        a_p = qk_p * decay
        t_p = _inv_unit_triangular_packed(l_p, m)
        k_st = jnp.concatenate([k] * N_STREAMS, axis=1)
        v_st = jnp.concatenate([v[:, :, :dk], v[:, :, dk:]] * (N_STREAMS // 2), axis=1)
        g_cst = jnp.concatenate(g_cols, axis=1)
        eg_cst = jnp.exp(g_cst)
        vk = jnp.concatenate([v_st, (k_st.astype(F32) * eg_cst).astype(BF16)], axis=2)
        uw = _bdot(m.bd(t_p * beta_p), vk).astype(BF16)
        a_uw = _bdot(m.bd(a_p), uw)
        q_st = jnp.concatenate([q] * N_STREAMS, axis=1).astype(F32)
        q_prime = q_st * eg_cst - a_uw[:, :, dk:]
        qo = jnp.concatenate([q_prime, a_uw[:, :, :dk]], axis=2).astype(qo_ref.dtype)
        gl_st = jnp.concatenate([jnp.broadcast_to(gl[:, s:s + 1, :], (g, c, dk)) for s in range(N_STREAMS)], axis=1)
        kt_st = (k_st.astype(F32) * jnp.exp(gl_st - g_cst)).astype(BF16)
        for s in range(N_STREAMS):
            seg = slice(s * c, (s + 1) * c)
            d, vh = divmod(s, 2)
            pr_ref[d, vh, pl.ds(j0, g)] = _bdot(kt_st[:, seg], uw[:, seg], 1, 1).astype(pr_ref.dtype)
            qo_ref[d, vh, rows, :] = qo[:, seg].reshape(g * c, 2 * dk)
        return carry

    lax.fori_loop(0, chunks // g, group_step, 0)


SCAN_HEADS = 8


def _gdn_scan_kernel(prf_ref, prb_ref, qof_ref, qob_ref, glf_ref, glb_ref, of_ref, ob_ref, s_ref, *, chunks):
    c, dk = CHUNK, HEAD_DIM

    @pl.when(pl.program_id(1) == 0)
    def _():
        s_ref[...] = jnp.zeros_like(s_ref)

    def chunk_step(j, carry):
        for d, (pr_ref, qo_ref, gl_ref, o_ref) in enumerate(((prf_ref, qof_ref, glf_ref, of_ref),
                                                             (prb_ref, qob_ref, glb_ref, ob_ref))):
            cj = j if d == 0 else chunks - 1 - j
            rows = pl.ds(pl.multiple_of(cj * c, c), c)
            for pair in range(SCAN_HEADS // 2):
                vhs = (2 * pair, 2 * pair + 1)
                prs = [pr_ref[vh, cj] for vh in vhs]
                qos = [qo_ref[vh, rows, :] for vh in vhs]
                ss = [s_ref[d * SCAN_HEADS + vh] for vh in vhs]
                lhs = jnp.concatenate([jnp.concatenate([pr[:, dk:], qo[:, :dk]], axis=0)
                                       for pr, qo in zip(prs, qos)], axis=1)
                zero = jnp.zeros((dk, dk), BF16)
                s_bd = jnp.concatenate([jnp.concatenate([ss[0].astype(BF16), zero], axis=1),
                                        jnp.concatenate([zero, ss[1].astype(BF16)], axis=1)], axis=0)
                x = jnp.dot(lhs, s_bd, preferred_element_type=F32)
                for n, vh in enumerate(vhs):
                    xs = x[:, n * dk:(n + 1) * dk]
                    o_ref[rows, vh * dk:(vh + 1) * dk] = (xs[dk:] + qos[n][:, dk:].astype(F32)).astype(o_ref.dtype)
                    s_ref[d * SCAN_HEADS + vh] = (ss[n] * jnp.exp(gl_ref[vh, cj])
                                                  + prs[n][:, :dk].astype(F32) - xs[:dk])
        return carry

    lax.fori_loop(0, chunks, chunk_step, 0)


def _gated_norm_kernel(of_ref, ob_ref, z_ref, nw_ref, y_ref):
    o = of_ref[...].astype(F32) + ob_ref[...].astype(F32)
    gate = _silu(z_ref[...].astype(F32))
    for h in range(o.shape[1] // HEAD_DIM):
        cols = slice(h * HEAD_DIM, (h + 1) * HEAD_DIM)
        y_ref[:, cols] = (_rms(o[:, cols]) * nw_ref[...] * gate[:, cols]).astype(y_ref.dtype)


PREP_GROUP = 8
PREP_CHUNKS = 16
SCAN_CHUNKS = 8


def gdn_core(qkv, proj, gates, norm_w, n_qk, n_v):
    t = qkv.shape[0]
    c, dk = CHUNK, HEAD_DIM
    dv = dk
    assert n_v == 2 * n_qk and t % c == 0 and n_v % SCAN_HEADS == 0
    nc = t // c
    qk_dim, v_dim = n_qk * dk, n_v * dv
    beta = gates[:, :2 * n_v].reshape(nc, c, 2, n_qk, 2)
    cum = gates[:, 2 * n_v:].reshape(nc, c, 2, n_qk, 2)
    packed = lambda a: a.transpose(3, 0, 2, 4, 1).reshape(n_qk, nc, 1, N_STREAMS * c)
    tab = jnp.concatenate([packed(beta), packed(cum), jnp.zeros((n_qk, nc, 6, N_STREAMS * c), F32)], axis=2)
    last = jnp.stack([cum[:, c - 1, 0], cum[:, 0, 1]])
    gl_prep = last.transpose(2, 1, 0, 3).reshape(n_qk, nc, N_STREAMS, 1)
    gl_prep = jnp.broadcast_to(jnp.pad(gl_prep, ((0, 0), (0, 0), (0, 8 - N_STREAMS), (0, 0))), (n_qk, nc, 8, dk))
    gl_scan = jnp.broadcast_to(last.reshape(2, nc, n_v).transpose(0, 2, 1)[..., None, None], (2, n_v, nc, 1, dv))

    pc = PREP_CHUNKS
    assert nc % pc == 0 and pc % PREP_GROUP == 0
    rb = pc * c
    pr, qo = pl.pallas_call(
        functools.partial(_gdn_prep_kernel, chunks=pc),
        grid=(n_qk, nc // pc),
        in_specs=[pl.BlockSpec((rb, dk), lambda h, b: (b, h)),
                  pl.BlockSpec((rb, dk), lambda h, b: (b, n_qk + h)),
                  pl.BlockSpec((rb, 2 * dv), lambda h, b: (b, qk_dim // dv + h)),
                  pl.BlockSpec((None, pc, 8, N_STREAMS * c), lambda h, b: (h, b, 0, 0)),
                  pl.BlockSpec((None, pc, 8, dk), lambda h, b: (h, b, 0, 0))],
        out_specs=[pl.BlockSpec((2, 2, pc, dk, 2 * dv), lambda h, b: (0, h, b, 0, 0)),
                   pl.BlockSpec((2, 2, rb, 2 * dv), lambda h, b: (0, h, b, 0))],
        out_shape=[jax.ShapeDtypeStruct((2, n_v, nc, dk, 2 * dv), BF16),
                   jax.ShapeDtypeStruct((2, n_v, t, 2 * dv), BF16)],
        compiler_params=_cparams(("arbitrary", "arbitrary")),
        name="gdn_prep",
    )(qkv, qkv, qkv, tab, gl_prep)

    sc = _pick(nc, (SCAN_CHUNKS, 4, 2, 1))
    ng = nc // sc
    rs = sc * c
    fwd = lambda *tail: (lambda g, b: (0, g, b) + tail)
    bwd = lambda *tail: (lambda g, b: (1, g, ng - 1 - b) + tail)
    o_f, o_b = pl.pallas_call(
        functools.partial(_gdn_scan_kernel, chunks=sc),
        grid=(n_v // SCAN_HEADS, ng),
        in_specs=[pl.BlockSpec((None, SCAN_HEADS, sc, dk, 2 * dv), fwd(0, 0)),
                  pl.BlockSpec((None, SCAN_HEADS, sc, dk, 2 * dv), bwd(0, 0)),
                  pl.BlockSpec((None, SCAN_HEADS, rs, 2 * dv), fwd(0)),
                  pl.BlockSpec((None, SCAN_HEADS, rs, 2 * dv), bwd(0)),
                  pl.BlockSpec((None, SCAN_HEADS, sc, 1, dv), fwd(0, 0)),
                  pl.BlockSpec((None, SCAN_HEADS, sc, 1, dv), bwd(0, 0))],
        out_specs=[pl.BlockSpec((rs, SCAN_HEADS * dv), lambda g, b: (b, g)),
                   pl.BlockSpec((rs, SCAN_HEADS * dv), lambda g, b: (ng - 1 - b, g))],
        out_shape=[jax.ShapeDtypeStruct((t, v_dim), BF16)] * 2,
        scratch_shapes=[pltpu.VMEM((2 * SCAN_HEADS, dk, dv), F32)],
        compiler_params=_cparams(("arbitrary", "arbitrary")),
        name="gdn_scan",
    )(pr, pr, qo, qo, gl_scan, gl_scan)

    tm = _pick(t, (512, 256, 128, 64))
    tc = _pick(v_dim, (1024, 512, 256, 128))
    z_off = (2 * qk_dim + v_dim) // tc
    tile = pl.BlockSpec((tm, tc), lambda i, j: (i, j))
    return pl.pallas_call(
        _gated_norm_kernel,
        grid=(t // tm, v_dim // tc),
        in_specs=[tile, tile, pl.BlockSpec((tm, tc), lambda i, j: (i, z_off + j)),
                  pl.BlockSpec((1, dv), lambda i, j: (0, 0))],
        out_specs=tile,
        out_shape=jax.ShapeDtypeStruct((t, v_dim), BF16),
        compiler_params=_cparams(("arbitrary", "arbitrary")),
        name="gdn_gated_norm",
    )(o_f, o_b, proj, norm_w.reshape(1, dv).astype(F32))


def kernel(x, c, ada_w, ada_b, norm_w, gdn_in_w, gdn_conv_w, gdn_A_log, gdn_dt_bias, gdn_norm_w, gdn_out_w, cf_pw1_w, cf_pw1_b, cf_dw_w, cf_dw_b, cf_ln_w, cf_ln_b, cf_pw2_w, cf_pw2_b, ffn_in_w, ffn_out_w):
    bsz, t, d = x.shape
    assert bsz == 1
    depth = ada_w.shape[0]
    n_v = gdn_A_log.shape[-1]
    n_qk = n_v // 2
    qk_dim, v_dim = n_qk * HEAD_DIM, n_v * HEAD_DIM
    conv_dim = 2 * qk_dim + v_dim
    row = lambda a: a.reshape(1, -1).astype(F32)
    zero_row = jnp.zeros((1, d), F32)

    mod = ada_modulation(c, ada_w, ada_b)[:, 0].reshape(depth, 6, d)
    xs = x[0]
    h = prenorm(xs, row(norm_w[0, 0]), row(mod[0, 1]), row(mod[0, 0]))
    for i in range(depth):
        j = i // N_MIXERS
        if i % N_MIXERS == 0:
            proj = matmul(h, gdn_in_w, j, 0, conv_dim + v_dim, BF16)
            ba = matmul(h, gdn_in_w, j, conv_dim + v_dim, 4 * n_v, F32)
            gates = gdn_gates(ba, gdn_A_log[j], gdn_dt_bias[j])
            qkv = qkv_conv(proj, gdn_conv_w[j].astype(F32), qk_dim)
            mix = gdn_core(qkv, proj, gates, gdn_norm_w[j], n_qk, n_v)
            w_out, b_out = gdn_out_w[j].astype(BF16), zero_row
        else:
            u = matmul_glu(h, cf_pw1_w, j, row(cf_pw1_b[j]), "glu")
            mix = conformer_conv_ln(u, cf_dw_w[j].astype(F32), row(cf_dw_b[j]), row(cf_ln_w[j]), row(cf_ln_b[j]))
            w_out, b_out = cf_pw2_w[j].astype(BF16), row(cf_pw2_b[j])
        xs, h = matmul_residual(mix, w_out, b_out, xs, row(norm_w[i, 1]), row(mod[i, 2]),
                                row(norm_w[i, 2]), row(mod[i, 4]), row(mod[i, 3]), True)
        f = matmul_glu(h, ffn_in_w, i, jnp.zeros((1, ffn_in_w.shape[2]), F32), "swiglu")
        last = i == depth - 1
        nxt = (zero_row, zero_row, zero_row) if last else (
            row(norm_w[i + 1, 0]), row(mod[i + 1, 1]), row(mod[i + 1, 0]))
        xs, h = matmul_residual(f, ffn_out_w[i].astype(BF16), zero_row, xs, row(norm_w[i, 3]),
                                row(mod[i, 5]), *nxt, not last)
    return xs[None]
```

```python
import functools

import jax
import jax.numpy as jnp
from jax import lax
from jax.experimental import pallas as pl
from jax.experimental.pallas import tpu as pltpu

F32 = jnp.float32
BF16 = jnp.bfloat16

HEAD_DIM = 128
N_MIXERS = 2
CHUNK = 64
RMS_EPS = 1e-6
LN_EPS = 1e-5
L2_EPS = 1e-6

V7X_VMEM_BYTES = 64 * 1024 * 1024
VMEM_LIMIT = V7X_VMEM_BYTES - 12 * 1024 * 1024
HALO = 16


def _cparams(sem):
    return pltpu.CompilerParams(dimension_semantics=sem, vmem_limit_bytes=VMEM_LIMIT)


def _pick(n, prefs):
    for p in prefs:
        if n % p == 0:
            return p
    return n


def _sigmoid(x):
    return 1.0 / (1.0 + jnp.exp(-x))


def _silu(x):
    return x * _sigmoid(x)


def _rms(y):
    return y * lax.rsqrt(jnp.mean(y * y, axis=-1, keepdims=True) + RMS_EPS)


def _ada_kernel(c_ref, w_ref, b_ref, o_ref):
    cond = _silu(c_ref[...])
    o_ref[...] = jnp.dot(cond.astype(BF16), w_ref[...].astype(BF16),
                         preferred_element_type=F32) + b_ref[...]


def ada_modulation(c, ada_w, ada_b):
    depth, d, n = ada_w.shape
    b = c.shape[0]
    assert b <= 8
    cp = jnp.zeros((8, d), F32).at[:b].set(c)
    tn = _pick(n, (1024, 512, 256, 128))
    out = pl.pallas_call(
        _ada_kernel,
        grid=(depth, n // tn),
        in_specs=[pl.BlockSpec((8, d), lambda l, j: (0, 0)),
                  pl.BlockSpec((None, d, tn), lambda l, j: (l, 0, j)),
                  pl.BlockSpec((None, 1, tn), lambda l, j: (l, 0, j))],
        out_specs=pl.BlockSpec((None, 8, tn), lambda l, j: (l, 0, j)),
        out_shape=jax.ShapeDtypeStruct((depth, 8, n), F32),
        compiler_params=_cparams(("arbitrary", "arbitrary")),
        name="ada_modulation",
    )(cp, ada_w, ada_b.reshape(depth, 1, n))
    return out[:, :b]


def _prenorm_kernel(x_ref, nw_ref, sc_ref, sh_ref, h_ref):
    y = _rms(x_ref[...])
    h_ref[...] = (y * nw_ref[...] * (1.0 + sc_ref[...]) + sh_ref[...]).astype(h_ref.dtype)


def prenorm(x, nw, sc, sh):
    t, d = x.shape
    tm = _pick(t, (512, 256, 128, 64, 8))
    row = pl.BlockSpec((1, d), lambda i: (0, 0))
    return pl.pallas_call(
        _prenorm_kernel,
        grid=(t // tm,),
        in_specs=[pl.BlockSpec((tm, d), lambda i: (i, 0)), row, row, row],
        out_specs=pl.BlockSpec((tm, d), lambda i: (i, 0)),
        out_shape=jax.ShapeDtypeStruct((t, d), BF16),
        compiler_params=_cparams(("arbitrary",)),
        name="prenorm",
    )(x, nw, sc, sh)


def _mm_kernel(x_ref, w_ref, o_ref, wb_ref):
    @pl.when(pl.program_id(1) == 0)
    def _():
        wb_ref[...] = w_ref[...].astype(BF16)

    o_ref[...] = jnp.dot(x_ref[...], wb_ref[...], preferred_element_type=F32).astype(o_ref.dtype)


def matmul(x, w, layer, col0, n, out_dtype, tm_prefs=(1024, 512, 256, 128, 64), tn_prefs=(1024, 512, 256, 128)):
    m, k = x.shape
    tm, tn = _pick(m, tm_prefs), _pick(n, tn_prefs)
    assert col0 % tn == 0
    j0 = col0 // tn
    return pl.pallas_call(
        _mm_kernel,
        grid=(n // tn, m // tm),
        in_specs=[pl.BlockSpec((tm, k), lambda j, i: (i, 0)),
                  pl.BlockSpec((None, k, tn), lambda j, i: (layer, 0, j0 + j))],
        out_specs=pl.BlockSpec((tm, tn), lambda j, i: (i, j)),
        out_shape=jax.ShapeDtypeStruct((m, n), out_dtype),
        scratch_shapes=[pltpu.VMEM((k, tn), BF16)],
        compiler_params=_cparams(("arbitrary", "arbitrary")),
        name="matmul",
    )(x, w)


def _mm_glu_kernel(x_ref, wa_ref, wb_ref, ba_ref, bb_ref, o_ref, wab_ref, wbb_ref, *, act):
    @pl.when(pl.program_id(1) == 0)
    def _():
        wab_ref[...] = wa_ref[...].astype(BF16)
        wbb_ref[...] = wb_ref[...].astype(BF16)

    x = x_ref[...]
    a = jnp.dot(x, wab_ref[...], preferred_element_type=F32) + ba_ref[...]
    b = jnp.dot(x, wbb_ref[...], preferred_element_type=F32) + bb_ref[...]
    if act == "swiglu":
        r = _silu(a) * b
    else:
        r = a * _sigmoid(b)
    o_ref[...] = r.astype(o_ref.dtype)


def matmul_glu(x, w, layer, bias, act, tm_prefs=(1024, 512, 256, 128, 64), tn_prefs=(512, 256, 128)):
    m, k = x.shape
    n = w.shape[2] // 2
    tm, tn = _pick(m, tm_prefs), _pick(n, tn_prefs)
    nb = n // tn
    return pl.pallas_call(
        functools.partial(_mm_glu_kernel, act=act),
        grid=(nb, m // tm),
        in_specs=[pl.BlockSpec((tm, k), lambda j, i: (i, 0)),
                  pl.BlockSpec((None, k, tn), lambda j, i: (layer, 0, j)),
                  pl.BlockSpec((None, k, tn), lambda j, i: (layer, 0, j + nb)),
                  pl.BlockSpec((1, tn), lambda j, i: (0, j)),
                  pl.BlockSpec((1, tn), lambda j, i: (0, j + nb))],
        out_specs=pl.BlockSpec((tm, tn), lambda j, i: (i, j)),
        out_shape=jax.ShapeDtypeStruct((m, n), BF16),
        scratch_shapes=[pltpu.VMEM((k, tn), BF16), pltpu.VMEM((k, tn), BF16)],
        compiler_params=_cparams(("arbitrary", "arbitrary")),
        name="matmul_" + act,
    )(x, w, w, bias, bias)


EPI_ROWS = 32


def _mm_res_kernel(a_ref, w_ref, b_ref, x_ref, nwp_ref, g_ref, nwn_ref, sc_ref, sh_ref,
                   xo_ref, *rest, nn):
    *maybe_ho_ref, y_even, y_odd = rest
    i, j = pl.program_id(0), pl.program_id(1)
    n_tiles = pl.num_programs(0) - 1
    tm, tn = y_even.shape[1], y_even.shape[2]
    d = nn * tn
    rows_per_step = tm // nn
    col = lambda ref, jj: ref[:, jj * tn:(jj + 1) * tn]

    @pl.when(jnp.logical_and(i == 0, j == 0))
    def _():
        y_odd[...] = jnp.zeros_like(y_odd)

    def epilogue_slice(y_prev):
        for r in range(rows_per_step // EPI_ROWS):
            rows = pl.ds(r * EPI_ROWS, EPI_ROWS)
            yrows = pl.ds(pl.multiple_of(j * rows_per_step + r * EPI_ROWS, EPI_ROWS), EPI_ROWS)
            ys = [y_prev[jj, yrows, :] + col(b_ref, jj) for jj in range(nn)]
            inv = lax.rsqrt(sum(jnp.sum(y * y, axis=-1, keepdims=True) for y in ys) / d + RMS_EPS)
            xns = [x_ref[rows, jj * tn:(jj + 1) * tn] + (1.0 + col(g_ref, jj)) * (ys[jj] * inv * col(nwp_ref, jj))
                   for jj in range(nn)]
            for jj in range(nn):
                xo_ref[rows, jj * tn:(jj + 1) * tn] = xns[jj]
            if maybe_ho_ref:
                inv2 = lax.rsqrt(sum(jnp.sum(v * v, axis=-1, keepdims=True) for v in xns) / d + RMS_EPS)
                for jj in range(nn):
                    hn = xns[jj] * inv2 * col(nwn_ref, jj) * (1.0 + col(sc_ref, jj)) + col(sh_ref, jj)
                    maybe_ho_ref[0][rows, jj * tn:(jj + 1) * tn] = hn.astype(BF16)

    def step(y_cur, y_prev):
        y_cur[j] = jnp.dot(a_ref[...], w_ref[...], preferred_element_type=F32)
        epilogue_slice(y_prev)

    even, busy = lax.rem(i, 2) == 0, i < n_tiles
    pl.when(jnp.logical_and(busy, even))(lambda: step(y_even, y_odd))
    pl.when(jnp.logical_and(busy, jnp.logical_not(even)))(lambda: step(y_odd, y_even))
    pl.when(jnp.logical_and(jnp.logical_not(busy), even))(lambda: epilogue_slice(y_odd))
    pl.when(jnp.logical_and(jnp.logical_not(busy), jnp.logical_not(even)))(lambda: epilogue_slice(y_even))


def matmul_residual(a, w, bias, x, nw_post, gate, nw_next, sc_next, sh_next, emit_next,
                    tm_prefs=(512, 256, 128, 64)):
    m, kdim = a.shape
    d = w.shape[1]
    tm = _pick(m, tm_prefs)
    tn = _pick(d, (1024, 512, 256, 128) if kdim <= 4096 else (512, 256, 128))
    nn = d // tn
    n_tiles = m // tm
    assert (tm // nn) % EPI_ROWS == 0
    row = pl.BlockSpec((1, d), lambda i, j: (0, 0))
    tile = pl.BlockSpec((tm // nn, d), lambda i, j: (jnp.maximum((i - 1) * nn + j, 0), 0))
    out_shape = [jax.ShapeDtypeStruct((m, d), F32)]
    out_specs = [tile]
    if emit_next:
        out_shape.append(jax.ShapeDtypeStruct((m, d), BF16))
        out_specs.append(tile)
    res = pl.pallas_call(
        functools.partial(_mm_res_kernel, nn=nn),
        grid=(n_tiles + 1, nn),
        in_specs=[pl.BlockSpec((tm, kdim), lambda i, j: (jnp.minimum(i, n_tiles - 1), 0)),
                  pl.BlockSpec((kdim, tn), lambda i, j: (0, j)),
                  row, tile, row, row, row, row, row],
        out_specs=out_specs,
        out_shape=out_shape,
        scratch_shapes=[pltpu.VMEM((nn, tm, tn), F32), pltpu.VMEM((nn, tm, tn), F32)],
        compiler_params=_cparams(("arbitrary", "arbitrary")),
        name="matmul_residual",
    )(a, w, bias, x, nw_post, gate, nw_next, sc_next, sh_next)
    return (res[0], res[1]) if emit_next else (res[0], None)


LANES = 128


def _halo_specs(tm, tc, t):
    per = tm // HALO
    last = t // HALO - 1
    return [pl.BlockSpec((HALO, tc), lambda i, j: (jnp.maximum(i * per - 1, 0), j)),
            pl.BlockSpec((tm, tc), lambda i, j: (i, j)),
            pl.BlockSpec((HALO, tc), lambda i, j: (jnp.minimum((i + 1) * per, last), j))]


def _fill_ext(up_ref, uc_ref, un_ref, ext_ref):
    i, n = pl.program_id(0), pl.num_programs(0)
    tm = uc_ref.shape[0]
    for s in range(ext_ref.shape[0]):
        cols = slice(s * LANES, (s + 1) * LANES)
        ext_ref[s, 0:HALO, :] = jnp.where(i > 0, up_ref[:, cols].astype(F32), 0.0)
        ext_ref[s, HALO:HALO + tm, :] = uc_ref[:, cols].astype(F32)
        ext_ref[s, HALO + tm:HALO + tm + HALO, :] = jnp.where(i < n - 1, un_ref[:, cols].astype(F32), 0.0)


def _conv_rows(ext_ref, w_ref, slab, base, rows, width):
    pad = width // 2
    cols = slice(slab * LANES, (slab + 1) * LANES)
    acc = None
    for j in range(width):
        term = ext_ref[slab, pl.ds(base + (HALO - pad + j), rows), :] * w_ref[j:j + 1, cols]
        acc = term if acc is None else acc + term
    return acc


CONF_ROWS = 64
LN_ROWS = 16


def _conf_conv_kernel(up_ref, uc_ref, un_ref, w_ref, b_ref, lnw_ref, lnb_ref, o_ref, ext_ref, cv_ref, *, width):
    _fill_ext(up_ref, uc_ref, un_ref, ext_ref)
    n_slabs = ext_ref.shape[0]
    d = n_slabs * LANES

    def step(r, carry):
        base = pl.multiple_of(r * CONF_ROWS, CONF_ROWS)
        for s in range(n_slabs):
            cols = slice(s * LANES, (s + 1) * LANES)
            cv_ref[:, cols] = _conv_rows(ext_ref, w_ref, s, base, CONF_ROWS, width) + b_ref[:, cols]
        for r0 in range(0, CONF_ROWS, LN_ROWS):
            u = cv_ref[r0:r0 + LN_ROWS, :]
            xc = u - jnp.mean(u, axis=-1, keepdims=True)
            y = xc * lax.rsqrt(jnp.mean(xc * xc, axis=-1, keepdims=True) + LN_EPS)
            y = y * lnw_ref[...] + lnb_ref[...]
            o_ref[pl.ds(base + r0, LN_ROWS), :] = _silu(y).astype(o_ref.dtype)
        return carry

    lax.fori_loop(0, uc_ref.shape[0] // CONF_ROWS, step, 0)


def conformer_conv_ln(u, dw_w, dw_b, ln_w, ln_b):
    t, d = u.shape
    width = dw_w.shape[0]
    assert width // 2 <= HALO and d % LANES == 0
    tm = _pick(t, (256, 128, 64))
    row = pl.BlockSpec((1, d), lambda i, j: (0, 0))
    return pl.pallas_call(
        functools.partial(_conf_conv_kernel, width=width),
        grid=(t // tm, 1),
        in_specs=_halo_specs(tm, d, t) + [pl.BlockSpec((width, d), lambda i, j: (0, 0)), row, row, row],
        out_specs=pl.BlockSpec((tm, d), lambda i, j: (i, 0)),
        out_shape=jax.ShapeDtypeStruct((t, d), BF16),
        scratch_shapes=[pltpu.VMEM((d // LANES, tm + 2 * HALO, LANES), F32), pltpu.VMEM((CONF_ROWS, d), F32)],
        compiler_params=_cparams(("arbitrary", "arbitrary")),
        name="conformer_conv_ln",
    )(u, u, u, dw_w, dw_b, ln_w, ln_b)


QKV_ROWS = 64


def _qkv_conv_kernel(up_ref, uc_ref, un_ref, w_ref, o_ref, ext_ref, *, width, q_tiles, qk_tiles, q_scale):
    _fill_ext(up_ref, uc_ref, un_ref, ext_ref)
    j = pl.program_id(1)
    scale = jnp.where(j < q_tiles, q_scale, 1.0).astype(F32)

    def make_step(normalise):
        def step(r, carry):
            base = pl.multiple_of(r * QKV_ROWS, QKV_ROWS)
            for s in range(ext_ref.shape[0]):
                y = _silu(_conv_rows(ext_ref, w_ref, s, base, QKV_ROWS, width))
                if normalise:
                    y = y * (lax.rsqrt(jnp.sum(y * y, axis=-1, keepdims=True) + L2_EPS) * scale)
                o_ref[pl.ds(base, QKV_ROWS), s * LANES:(s + 1) * LANES] = y.astype(o_ref.dtype)
            return carry
        return step

    nsteps = uc_ref.shape[0] // QKV_ROWS

    @pl.when(j < qk_tiles)
    def _():
        lax.fori_loop(0, nsteps, make_step(True), 0)

    @pl.when(j >= qk_tiles)
    def _():
        lax.fori_loop(0, nsteps, make_step(False), 0)


def qkv_conv(proj, conv_w, qk_dim):
    t = proj.shape[0]
    width, cdim = conv_w.shape
    assert HEAD_DIM == LANES and width // 2 <= HALO
    tm = _pick(t, (512, 256, 128, 64, 32))
    tc = _pick(qk_dim, (1024, 512, 256, 128))
    return pl.pallas_call(
        functools.partial(_qkv_conv_kernel, width=width, q_tiles=qk_dim // tc,
                          qk_tiles=2 * qk_dim // tc, q_scale=HEAD_DIM ** -0.5),
        grid=(t // tm, cdim // tc),
        in_specs=_halo_specs(tm, tc, t) + [pl.BlockSpec((width, tc), lambda i, j: (0, j))],
        out_specs=pl.BlockSpec((tm, tc), lambda i, j: (i, j)),
        out_shape=jax.ShapeDtypeStruct((t, cdim), BF16),
        scratch_shapes=[pltpu.VMEM((tc // LANES, tm + 2 * HALO, LANES), F32)],
        compiler_params=_cparams(("arbitrary", "arbitrary")),
        name="qkv_conv",
    )(proj, proj, proj, conv_w)


def _split3(x):
    hi = x.astype(BF16)
    r = x - hi.astype(F32)
    mid = r.astype(BF16)
    lo = (r - mid.astype(F32)).astype(BF16)
    return hi, mid, lo


def _gate_kernel(ba_ref, alog_ref, dt_ref, o_ref, *, n_heads):
    tm = ba_ref.shape[0]
    ri = lax.broadcasted_iota(jnp.int32, (CHUNK, CHUNK), 0)
    ci = lax.broadcasted_iota(jnp.int32, (CHUNK, CHUNK), 1)
    tril = jnp.where(ri >= ci, 1.0, 0.0).astype(BF16)
    triu = jnp.where(ri <= ci, 1.0, 0.0).astype(BF16)
    lane = lax.broadcasted_iota(jnp.int32, (CHUNK, 4 * n_heads), 1)
    for c in range(tm // CHUNK):
        ba = ba_ref[c * CHUNK:(c + 1) * CHUNK, :]
        beta = _sigmoid(ba)
        x = ba + dt_ref[...]
        softplus = jnp.maximum(x, 0.0) + jnp.log(1.0 + jnp.exp(-jnp.abs(x)))
        g = -jnp.exp(alog_ref[...]) * softplus
        pre = jnp.zeros_like(g)
        suf = jnp.zeros_like(g)
        for part in _split3(g):
            pre = pre + jnp.dot(tril, part, preferred_element_type=F32)
            suf = suf + jnp.dot(triu, part, preferred_element_type=F32)
        cum = jnp.where(lane < 3 * n_heads, pre, suf)
        o_ref[c * CHUNK:(c + 1) * CHUNK, :] = jnp.where(lane < 2 * n_heads, beta, cum)


def gdn_gates(ba, a_log, dt_bias):
    t, n = ba.shape
    nv = n // 4
    tm = _pick(t, (512, 256, 128, 64))
    zeros = jnp.zeros((2 * nv,), F32)
    alog_row = jnp.concatenate([zeros, a_log.reshape(-1).astype(F32)]).reshape(1, n)
    dt_row = jnp.concatenate([zeros, dt_bias.reshape(-1).astype(F32)]).reshape(1, n)
    row = pl.BlockSpec((1, n), lambda i: (0, 0))
    return pl.pallas_call(
        functools.partial(_gate_kernel, n_heads=nv),
        grid=(t // tm,),
        in_specs=[pl.BlockSpec((tm, n), lambda i: (i, 0)), row, row],
        out_specs=pl.BlockSpec((tm, n), lambda i: (i, 0)),
        out_shape=jax.ShapeDtypeStruct((t, n), F32),
        compiler_params=_cparams(("arbitrary",)),
        name="gdn_gates",
    )(ba, alog_row, dt_row)


def _bdot(a, b, ca=2, cb=1):
    return lax.dot_general(a.astype(BF16), b.astype(BF16), (((ca,), (cb,)), ((0,), (0,))),
                           preferred_element_type=F32)


N_STREAMS = 4


class _PackedMasks:
    def __init__(self):
        c, w = CHUNK, N_STREAMS * CHUNK
        ri = lax.broadcasted_iota(jnp.int32, (c, w), 0)
        li = lax.broadcasted_iota(jnp.int32, (c, w), 1)
        ci = jnp.bitwise_and(li, c - 1)
        fwd = li < (N_STREAMS // 2) * c
        self.eye = ri == ci
        bwd = jnp.logical_not(fwd)
        self.incl = jnp.logical_or(jnp.logical_and(fwd, ri >= ci), jnp.logical_and(bwd, ri <= ci))
        self.strict = jnp.logical_or(jnp.logical_and(fwd, ri > ci), jnp.logical_and(bwd, ri < ci))
        self.seg = [jnp.right_shift(li, 6) == s for s in range(N_STREAMS)]
        self.same16 = jnp.right_shift(ri, 4) == jnp.right_shift(ci, 4)
        self.same32 = jnp.right_shift(ri, 5) == jnp.right_shift(ci, 5)
        rb = lax.broadcasted_iota(jnp.int32, (w, w), 0)
        lb = lax.broadcasted_iota(jnp.int32, (w, w), 1)
        self.blockdiag = jnp.right_shift(rb, 6) == jnp.right_shift(lb, 6)


    def spread(self, cols):
        out = cols[-1]
        for s in range(N_STREAMS - 2, -1, -1):
            out = jnp.where(self.seg[s], cols[s], out)
        return out

    def columns(self, row):
        z = jnp.where(self.eye, row, 0.0)
        return [jnp.sum(jnp.where(self.seg[s], z, 0.0), axis=-1, keepdims=True) for s in range(N_STREAMS)]

    def bd(self, xp):
        return jnp.where(self.blockdiag, jnp.concatenate([xp] * N_STREAMS, axis=-2), 0.0).astype(BF16)


def _inv_unit_triangular_packed(lp, m):
    mm = lambda xp, yp: _bdot(xp, m.bd(yp))
    l16 = jnp.where(m.same16, lp, 0.0)
    c32 = jnp.where(jnp.logical_and(m.same32, jnp.logical_not(m.same16)), lp, 0.0)
    c64 = jnp.where(m.same32, 0.0, lp)
    p = jnp.where(m.eye, 1.0, 0.0) - l16
    sq = mm(l16, l16)
    for _ in range(2):
        both = mm(jnp.concatenate([p, sq], axis=-2), sq)
        p, sq = p + both[:, :CHUNK], both[:, CHUNK:]
    p = p + mm(p, sq)
    y = p - mm(mm(p, c32), p)
    return y - mm(mm(y, c64), y)


def _gdn_prep_kernel(q_ref, k_ref, v_ref, tab_ref, gl_ref, pr_ref, qo_ref, *, chunks):
    c, dk, g = CHUNK, HEAD_DIM, PREP_GROUP
    m = _PackedMasks()

    def group_step(i, carry):
        j0 = pl.multiple_of(i * g, g)
        rows = pl.ds(pl.multiple_of(i * (g * c), g * c), g * c)
        q = q_ref[rows, :].reshape(g, c, dk)
        k = k_ref[rows, :].reshape(g, c, dk)
        v = v_ref[rows, :].reshape(g, c, 2 * dk)
        tab, gl = tab_ref[pl.ds(j0, g)], gl_ref[pl.ds(j0, g)]
        beta_p, g_p = tab[:, 0:1, :], tab[:, 1:2, :]
        x = _bdot(jnp.concatenate([q, k], axis=1), jnp.concatenate([k, k], axis=1), 2, 2)
        qk_p = jnp.concatenate([x[:, :c], x[:, :c]], axis=2)
        kk_p = jnp.concatenate([x[:, c:], x[:, c:]], axis=2)
        g_cols = m.columns(g_p)
        g_cb = m.spread(g_cols)
        beta_cb = m.spread(m.columns(beta_p))
        decay = jnp.exp(jnp.where(m.incl, g_cb - g_p, -jnp.inf))
        l_p = jnp.where(m.strict, kk_p * decay * beta_cb, 0.0)
        a_p = qk_p * decay
        t_p = _inv_unit_triangular_packed(l_p, m)
        k_st = jnp.concatenate([k] * N_STREAMS, axis=1)
        v_st = jnp.concatenate([v[:, :, :dk], v[:, :, dk:]] * (N_STREAMS // 2), axis=1)
        g_cst = jnp.concatenate(g_cols, axis=1)
        eg_cst = jnp.exp(g_cst)
        vk = jnp.concatenate([v_st, (k_st.astype(F32) * eg_cst).astype(BF16)], axis=2)
        uw = _bdot(m.bd(t_p * beta_p), vk).astype(BF16)
        a_uw = _bdot(m.bd(a_p), uw)
        q_st = jnp.concatenate([q] * N_STREAMS, axis=1).astype(F32)
        q_prime = q_st * eg_cst - a_uw[:, :, dk:]
        qo = jnp.concatenate([q_prime, a_uw[:, :, :dk]], axis=2).astype(qo_ref.dtype)
        gl_st = jnp.concatenate([jnp.broadcast_to(gl[:, s:s + 1, :], (g, c, dk)) for s in range(N_STREAMS)], axis=1)
        kt_st = (k_st.astype(F32) * jnp.exp(gl_st - g_cst)).astype(BF16)
        for s in range(N_STREAMS):
            seg = slice(s * c, (s + 1) * c)
            d, vh = divmod(s, 2)
            pr_ref[d, vh, pl.ds(j0, g)] = _bdot(kt_st[:, seg], uw[:, seg], 1, 1).astype(pr_ref.dtype)
            qo_ref[d, vh, rows, :] = qo[:, seg].reshape(g * c, 2 * dk)
        return carry

    lax.fori_loop(0, chunks // g, group_step, 0)


SCAN_HEADS = 16


def _gdn_scan_kernel(pr_ref, qo_ref, gl_ref, *rest, chunks, reverse):
    c, dk = CHUNK, HEAD_DIM
    if reverse:
        of_ref, z_ref, nw_ref, out_ref, s_ref = rest
    else:
        out_ref, s_ref = rest

    @pl.when(pl.program_id(1) == 0)
    def _():
        s_ref[...] = jnp.zeros_like(s_ref)

    def chunk_step(j, carry):
        cj = chunks - 1 - j if reverse else j
        rows = pl.ds(pl.multiple_of(cj * c, c), c)
        for pair in range(SCAN_HEADS // 2):
            vhs = (2 * pair, 2 * pair + 1)
            prs = [pr_ref[vh, cj] for vh in vhs]
            qos = [qo_ref[vh, rows, :] for vh in vhs]
            ss = [s_ref[vh] for vh in vhs]
            lhs = jnp.concatenate([jnp.concatenate([pr[:, dk:], qo[:, :dk]], axis=0)
                                   for pr, qo in zip(prs, qos)], axis=1)
            zero = jnp.zeros((dk, dk), BF16)
            s_bd = jnp.concatenate([jnp.concatenate([ss[0].astype(BF16), zero], axis=1),
                                    jnp.concatenate([zero, ss[1].astype(BF16)], axis=1)], axis=0)
            x = jnp.dot(lhs, s_bd, preferred_element_type=F32)
            for n, vh in enumerate(vhs):
                xs = x[:, n * dk:(n + 1) * dk]
                cols = slice(vh * dk, (vh + 1) * dk)
                o = xs[dk:] + qos[n][:, dk:].astype(F32)
                if reverse:
                    tot = o + of_ref[rows, cols].astype(F32)
                    o = _rms(tot) * nw_ref[...] * _silu(z_ref[rows, cols].astype(F32))
                out_ref[rows, cols] = o.astype(out_ref.dtype)
                s_ref[vh] = ss[n] * jnp.exp(gl_ref[vh, cj]) + prs[n][:, :dk].astype(F32) - xs[:dk]
        return carry

    lax.fori_loop(0, chunks, chunk_step, 0)


PREP_GROUP = 8
PREP_CHUNKS = 16
SCAN_CHUNKS = 8


def gdn_core(qkv, proj, gates, norm_w, n_qk, n_v):
    t = qkv.shape[0]
    c, dk = CHUNK, HEAD_DIM
    dv = dk
    assert n_v == 2 * n_qk and t % c == 0 and n_v % SCAN_HEADS == 0
    nc = t // c
    qk_dim, v_dim = n_qk * dk, n_v * dv
    beta = gates[:, :2 * n_v].reshape(nc, c, 2, n_qk, 2)
    cum = gates[:, 2 * n_v:].reshape(nc, c, 2, n_qk, 2)
    packed = lambda a: a.transpose(3, 0, 2, 4, 1).reshape(n_qk, nc, 1, N_STREAMS * c)
    tab = jnp.concatenate([packed(beta), packed(cum), jnp.zeros((n_qk, nc, 6, N_STREAMS * c), F32)], axis=2)
    last = jnp.stack([cum[:, c - 1, 0], cum[:, 0, 1]])
    gl_prep = last.transpose(2, 1, 0, 3).reshape(n_qk, nc, N_STREAMS, 1)
    gl_prep = jnp.broadcast_to(jnp.pad(gl_prep, ((0, 0), (0, 0), (0, 8 - N_STREAMS), (0, 0))), (n_qk, nc, 8, dk))
    gl_scan = jnp.broadcast_to(last.reshape(2, nc, n_v).transpose(0, 2, 1)[..., None, None], (2, n_v, nc, 1, dv))

    pc = PREP_CHUNKS
    assert nc % pc == 0 and pc % PREP_GROUP == 0
    rb = pc * c
    pr, qo = pl.pallas_call(
        functools.partial(_gdn_prep_kernel, chunks=pc),
        grid=(n_qk, nc // pc),
        in_specs=[pl.BlockSpec((rb, dk), lambda h, b: (b, h)),
                  pl.BlockSpec((rb, dk), lambda h, b: (b, n_qk + h)),
                  pl.BlockSpec((rb, 2 * dv), lambda h, b: (b, qk_dim // dv + h)),
                  pl.BlockSpec((None, pc, 8, N_STREAMS * c), lambda h, b: (h, b, 0, 0)),
                  pl.BlockSpec((None, pc, 8, dk), lambda h, b: (h, b, 0, 0))],
        out_specs=[pl.BlockSpec((2, 2, pc, dk, 2 * dv), lambda h, b: (0, h, b, 0, 0)),
                   pl.BlockSpec((2, 2, rb, 2 * dv), lambda h, b: (0, h, b, 0))],
        out_shape=[jax.ShapeDtypeStruct((2, n_v, nc, dk, 2 * dv), BF16),
                   jax.ShapeDtypeStruct((2, n_v, t, 2 * dv), BF16)],
        compiler_params=_cparams(("arbitrary", "arbitrary")),
        name="gdn_prep",
    )(qkv, qkv, qkv, tab, gl_prep)

    sc = _pick(nc, (SCAN_CHUNKS, 4, 2, 1))
    ng = nc // sc
    rs = sc * c
    wide = SCAN_HEADS * dv
    z_off = (2 * qk_dim + v_dim) // wide

    def scan(reverse, extra_args, extra_specs):
        d = int(reverse)
        blk = (lambda b: ng - 1 - b) if reverse else (lambda b: b)
        tile = pl.BlockSpec((rs, wide), lambda g, b: (blk(b), g))
        return pl.pallas_call(
            functools.partial(_gdn_scan_kernel, chunks=sc, reverse=reverse),
            grid=(n_v // SCAN_HEADS, ng),
            in_specs=[pl.BlockSpec((None, SCAN_HEADS, sc, dk, 2 * dv), lambda g, b: (d, g, blk(b), 0, 0)),
                      pl.BlockSpec((None, SCAN_HEADS, rs, 2 * dv), lambda g, b: (d, g, blk(b), 0)),
                      pl.BlockSpec((None, SCAN_HEADS, sc, 1, dv), lambda g, b: (d, g, blk(b), 0, 0))]
                     + [tile if spec is None else spec(blk) for spec in extra_specs],
            out_specs=tile,
            out_shape=jax.ShapeDtypeStruct((t, v_dim), BF16),
            scratch_shapes=[pltpu.VMEM((SCAN_HEADS, dk, dv), F32)],
            compiler_params=_cparams(("arbitrary", "arbitrary")),
            name="gdn_scan_bwd" if reverse else "gdn_scan_fwd",
        )(pr, qo, gl_scan, *extra_args)

    o_f = scan(False, (), ())
    z_spec = lambda blk: pl.BlockSpec((rs, wide), lambda g, b: (blk(b), z_off + g))
    nw_spec = lambda blk: pl.BlockSpec((1, dv), lambda g, b: (0, 0))
    return scan(True, (o_f, proj, norm_w.reshape(1, dv).astype(F32)), (None, z_spec, nw_spec))


def kernel(x, c, ada_w, ada_b, norm_w, gdn_in_w, gdn_conv_w, gdn_A_log, gdn_dt_bias, gdn_norm_w, gdn_out_w, cf_pw1_w, cf_pw1_b, cf_dw_w, cf_dw_b, cf_ln_w, cf_ln_b, cf_pw2_w, cf_pw2_b, ffn_in_w, ffn_out_w):
    bsz, t, d = x.shape
    assert bsz == 1
    depth = ada_w.shape[0]
    n_v = gdn_A_log.shape[-1]
    n_qk = n_v // 2
    qk_dim, v_dim = n_qk * HEAD_DIM, n_v * HEAD_DIM
    conv_dim = 2 * qk_dim + v_dim
    row = lambda a: a.reshape(1, -1).astype(F32)
    zero_row = jnp.zeros((1, d), F32)

    mod = ada_modulation(c, ada_w, ada_b)[:, 0].reshape(depth, 6, d)
    xs = x[0]
    h = prenorm(xs, row(norm_w[0, 0]), row(mod[0, 1]), row(mod[0, 0]))
    for i in range(depth):
        j = i // N_MIXERS
        if i % N_MIXERS == 0:
            proj = matmul(h, gdn_in_w, j, 0, conv_dim + v_dim, BF16)
            ba = matmul(h, gdn_in_w, j, conv_dim + v_dim, 4 * n_v, F32)
            gates = gdn_gates(ba, gdn_A_log[j], gdn_dt_bias[j])
            qkv = qkv_conv(proj, gdn_conv_w[j].astype(F32), qk_dim)
            mix = gdn_core(qkv, proj, gates, gdn_norm_w[j], n_qk, n_v)
            w_out, b_out = gdn_out_w[j].astype(BF16), zero_row
        else:
            u = matmul_glu(h, cf_pw1_w, j, row(cf_pw1_b[j]), "glu")
            mix = conformer_conv_ln(u, cf_dw_w[j].astype(F32), row(cf_dw_b[j]), row(cf_ln_w[j]), row(cf_ln_b[j]))
            w_out, b_out = cf_pw2_w[j].astype(BF16), row(cf_pw2_b[j])
        xs, h = matmul_residual(mix, w_out, b_out, xs, row(norm_w[i, 1]), row(mod[i, 2]),
                                row(norm_w[i, 2]), row(mod[i, 4]), row(mod[i, 3]), True)
        f = matmul_glu(h, ffn_in_w, i, jnp.zeros((1, ffn_in_w.shape[2]), F32), "swiglu")
        last = i == depth - 1
        nxt = (zero_row, zero_row, zero_row) if last else (
            row(norm_w[i + 1, 0]), row(mod[i + 1, 1]), row(mod[i + 1, 0]))
        xs, h = matmul_residual(f, ffn_out_w[i].astype(BF16), zero_row, xs, row(norm_w[i, 3]),
                                row(mod[i, 5]), *nxt, not last)
    return xs[None]
```

```python
import functools

import jax
import jax.numpy as jnp
from jax import lax
from jax.experimental import pallas as pl
from jax.experimental.pallas import tpu as pltpu

F32 = jnp.float32
BF16 = jnp.bfloat16

HEAD_DIM = 128
N_MIXERS = 2
CHUNK = 64
RMS_EPS = 1e-6
LN_EPS = 1e-5
L2_EPS = 1e-6

V7X_VMEM_BYTES = 64 * 1024 * 1024
VMEM_LIMIT = V7X_VMEM_BYTES - 12 * 1024 * 1024
HALO = 16


def _cparams(sem):
    return pltpu.CompilerParams(dimension_semantics=sem, vmem_limit_bytes=VMEM_LIMIT)


def _pick(n, prefs):
    for p in prefs:
        if n % p == 0:
            return p
    return n


def _sigmoid(x):
    return 1.0 / (1.0 + jnp.exp(-x))


def _silu(x):
    return x * _sigmoid(x)


def _rms(y):
    return y * lax.rsqrt(jnp.mean(y * y, axis=-1, keepdims=True) + RMS_EPS)


def _ada_kernel(c_ref, w_ref, b_ref, o_ref):
    cond = _silu(c_ref[...])
    o_ref[...] = jnp.dot(cond.astype(BF16), w_ref[...].astype(BF16),
                         preferred_element_type=F32) + b_ref[...]


def ada_modulation(c, ada_w, ada_b):
    depth, d, n = ada_w.shape
    b = c.shape[0]
    assert b <= 8
    cp = jnp.zeros((8, d), F32).at[:b].set(c)
    tn = _pick(n, (1024, 512, 256, 128))
    out = pl.pallas_call(
        _ada_kernel,
        grid=(depth, n // tn),
        in_specs=[pl.BlockSpec((8, d), lambda l, j: (0, 0)),
                  pl.BlockSpec((None, d, tn), lambda l, j: (l, 0, j)),
                  pl.BlockSpec((None, 1, tn), lambda l, j: (l, 0, j))],
        out_specs=pl.BlockSpec((None, 8, tn), lambda l, j: (l, 0, j)),
        out_shape=jax.ShapeDtypeStruct((depth, 8, n), F32),
        compiler_params=_cparams(("arbitrary", "arbitrary")),
        name="ada_modulation",
    )(cp, ada_w, ada_b.reshape(depth, 1, n))
    return out[:, :b]


def _prenorm_kernel(x_ref, nw_ref, sc_ref, sh_ref, h_ref):
    y = _rms(x_ref[...])
    h_ref[...] = (y * nw_ref[...] * (1.0 + sc_ref[...]) + sh_ref[...]).astype(h_ref.dtype)


def prenorm(x, nw, sc, sh):
    t, d = x.shape
    tm = _pick(t, (512, 256, 128, 64, 8))
    row = pl.BlockSpec((1, d), lambda i: (0, 0))
    return pl.pallas_call(
        _prenorm_kernel,
        grid=(t // tm,),
        in_specs=[pl.BlockSpec((tm, d), lambda i: (i, 0)), row, row, row],
        out_specs=pl.BlockSpec((tm, d), lambda i: (i, 0)),
        out_shape=jax.ShapeDtypeStruct((t, d), BF16),
        compiler_params=_cparams(("arbitrary",)),
        name="prenorm",
    )(x, nw, sc, sh)


def _mm_kernel(x_ref, w_ref, o_ref, wb_ref):
    @pl.when(pl.program_id(1) == 0)
    def _():
        wb_ref[...] = w_ref[...].astype(BF16)

    o_ref[...] = jnp.dot(x_ref[...], wb_ref[...], preferred_element_type=F32).astype(o_ref.dtype)


def matmul(x, w, layer, col0, n, out_dtype, tm_prefs=(1024, 512, 256, 128, 64), tn_prefs=(1024, 512, 256, 128)):
    m, k = x.shape
    tm, tn = _pick(m, tm_prefs), _pick(n, tn_prefs)
    assert col0 % tn == 0
    j0 = col0 // tn
    return pl.pallas_call(
        _mm_kernel,
        grid=(n // tn, m // tm),
        in_specs=[pl.BlockSpec((tm, k), lambda j, i: (i, 0)),
                  pl.BlockSpec((None, k, tn), lambda j, i: (layer, 0, j0 + j))],
        out_specs=pl.BlockSpec((tm, tn), lambda j, i: (i, j)),
        out_shape=jax.ShapeDtypeStruct((m, n), out_dtype),
        scratch_shapes=[pltpu.VMEM((k, tn), BF16)],
        compiler_params=_cparams(("arbitrary", "arbitrary")),
        name="matmul",
    )(x, w)


def _mm_glu_kernel(x_ref, wa_ref, wb_ref, ba_ref, bb_ref, o_ref, wab_ref, wbb_ref, *, act):
    @pl.when(pl.program_id(1) == 0)
    def _():
        wab_ref[...] = wa_ref[...].astype(BF16)
        wbb_ref[...] = wb_ref[...].astype(BF16)

    x = x_ref[...]
    a = jnp.dot(x, wab_ref[...], preferred_element_type=F32) + ba_ref[...]
    b = jnp.dot(x, wbb_ref[...], preferred_element_type=F32) + bb_ref[...]
    if act == "swiglu":
        r = _silu(a) * b
    else:
        r = a * _sigmoid(b)
    o_ref[...] = r.astype(o_ref.dtype)


def matmul_glu(x, w, layer, bias, act, tm_prefs=(1024, 512, 256, 128, 64), tn_prefs=(512, 256, 128)):
    m, k = x.shape
    n = w.shape[2] // 2
    tm, tn = _pick(m, tm_prefs), _pick(n, tn_prefs)
    nb = n // tn
    return pl.pallas_call(
        functools.partial(_mm_glu_kernel, act=act),
        grid=(nb, m // tm),
        in_specs=[pl.BlockSpec((tm, k), lambda j, i: (i, 0)),
                  pl.BlockSpec((None, k, tn), lambda j, i: (layer, 0, j)),
                  pl.BlockSpec((None, k, tn), lambda j, i: (layer, 0, j + nb)),
                  pl.BlockSpec((1, tn), lambda j, i: (0, j)),
                  pl.BlockSpec((1, tn), lambda j, i: (0, j + nb))],
        out_specs=pl.BlockSpec((tm, tn), lambda j, i: (i, j)),
        out_shape=jax.ShapeDtypeStruct((m, n), BF16),
        scratch_shapes=[pltpu.VMEM((k, tn), BF16), pltpu.VMEM((k, tn), BF16)],
        compiler_params=_cparams(("arbitrary", "arbitrary")),
        name="matmul_" + act,
    )(x, w, w, bias, bias)


EPI_ROWS = 32


def _mm_res_kernel(a_ref, w_ref, b_ref, x_ref, nwp_ref, g_ref, nwn_ref, sc_ref, sh_ref,
                   xo_ref, *rest, nn):
    *maybe_ho_ref, y_even, y_odd = rest
    i, j = pl.program_id(0), pl.program_id(1)
    n_tiles = pl.num_programs(0) - 1
    tm, tn = y_even.shape[1], y_even.shape[2]
    d = nn * tn
    rows_per_step = tm // nn
    col = lambda ref, jj: ref[:, jj * tn:(jj + 1) * tn]

    @pl.when(jnp.logical_and(i == 0, j == 0))
    def _():
        y_odd[...] = jnp.zeros_like(y_odd)

    def epilogue_slice(y_prev):
        for r in range(rows_per_step // EPI_ROWS):
            rows = pl.ds(r * EPI_ROWS, EPI_ROWS)
            yrows = pl.ds(pl.multiple_of(j * rows_per_step + r * EPI_ROWS, EPI_ROWS), EPI_ROWS)
            ys = [y_prev[jj, yrows, :] + col(b_ref, jj) for jj in range(nn)]
            inv = lax.rsqrt(sum(jnp.sum(y * y, axis=-1, keepdims=True) for y in ys) / d + RMS_EPS)
            xns = [x_ref[rows, jj * tn:(jj + 1) * tn] + (1.0 + col(g_ref, jj)) * (ys[jj] * inv * col(nwp_ref, jj))
                   for jj in range(nn)]
            for jj in range(nn):
                xo_ref[rows, jj * tn:(jj + 1) * tn] = xns[jj]
            if maybe_ho_ref:
                inv2 = lax.rsqrt(sum(jnp.sum(v * v, axis=-1, keepdims=True) for v in xns) / d + RMS_EPS)
                for jj in range(nn):
                    hn = xns[jj] * inv2 * col(nwn_ref, jj) * (1.0 + col(sc_ref, jj)) + col(sh_ref, jj)
                    maybe_ho_ref[0][rows, jj * tn:(jj + 1) * tn] = hn.astype(BF16)

    def step(y_cur, y_prev):
        y_cur[j] = jnp.dot(a_ref[...], w_ref[...], preferred_element_type=F32)
        epilogue_slice(y_prev)

    even, busy = lax.rem(i, 2) == 0, i < n_tiles
    pl.when(jnp.logical_and(busy, even))(lambda: step(y_even, y_odd))
    pl.when(jnp.logical_and(busy, jnp.logical_not(even)))(lambda: step(y_odd, y_even))
    pl.when(jnp.logical_and(jnp.logical_not(busy), even))(lambda: epilogue_slice(y_odd))
    pl.when(jnp.logical_and(jnp.logical_not(busy), jnp.logical_not(even)))(lambda: epilogue_slice(y_even))


def matmul_residual(a, w, bias, x, nw_post, gate, nw_next, sc_next, sh_next, emit_next,
                    tm_prefs=(512, 256, 128, 64)):
    m, kdim = a.shape
    d = w.shape[1]
    tm = _pick(m, tm_prefs)
    tn = _pick(d, (1024, 512, 256, 128) if kdim <= 4096 else (512, 256, 128))
    nn = d // tn
    n_tiles = m // tm
    assert (tm // nn) % EPI_ROWS == 0
    row = pl.BlockSpec((1, d), lambda i, j: (0, 0))
    tile = pl.BlockSpec((tm // nn, d), lambda i, j: (jnp.maximum((i - 1) * nn + j, 0), 0))
    out_shape = [jax.ShapeDtypeStruct((m, d), F32)]
    out_specs = [tile]
    if emit_next:
        out_shape.append(jax.ShapeDtypeStruct((m, d), BF16))
        out_specs.append(tile)
    res = pl.pallas_call(
        functools.partial(_mm_res_kernel, nn=nn),
        grid=(n_tiles + 1, nn),
        in_specs=[pl.BlockSpec((tm, kdim), lambda i, j: (jnp.minimum(i, n_tiles - 1), 0)),
                  pl.BlockSpec((kdim, tn), lambda i, j: (0, j)),
                  row, tile, row, row, row, row, row],
        out_specs=out_specs,
        out_shape=out_shape,
        scratch_shapes=[pltpu.VMEM((nn, tm, tn), F32), pltpu.VMEM((nn, tm, tn), F32)],
        compiler_params=_cparams(("arbitrary", "arbitrary")),
        name="matmul_residual",
    )(a, w, bias, x, nw_post, gate, nw_next, sc_next, sh_next)
    return (res[0], res[1]) if emit_next else (res[0], None)


LANES = 128


def _halo_specs(tm, tc, t):
    per = tm // HALO
    last = t // HALO - 1
    return [pl.BlockSpec((HALO, tc), lambda i, j: (jnp.maximum(i * per - 1, 0), j)),
            pl.BlockSpec((tm, tc), lambda i, j: (i, j)),
            pl.BlockSpec((HALO, tc), lambda i, j: (jnp.minimum((i + 1) * per, last), j))]


def _fill_ext(up_ref, uc_ref, un_ref, ext_ref):
    i, n = pl.program_id(0), pl.num_programs(0)
    tm = uc_ref.shape[0]
    for s in range(ext_ref.shape[0]):
        cols = slice(s * LANES, (s + 1) * LANES)
        ext_ref[s, 0:HALO, :] = jnp.where(i > 0, up_ref[:, cols].astype(F32), 0.0)
        ext_ref[s, HALO:HALO + tm, :] = uc_ref[:, cols].astype(F32)
        ext_ref[s, HALO + tm:HALO + tm + HALO, :] = jnp.where(i < n - 1, un_ref[:, cols].astype(F32), 0.0)


def _conv_rows(ext_ref, w_ref, slab, base, rows, width):
    pad = width // 2
    cols = slice(slab * LANES, (slab + 1) * LANES)
    acc = None
    for j in range(width):
        term = ext_ref[slab, pl.ds(base + (HALO - pad + j), rows), :] * w_ref[j:j + 1, cols]
        acc = term if acc is None else acc + term
    return acc


CONF_ROWS = 64
LN_ROWS = 16


def _conf_conv_kernel(up_ref, uc_ref, un_ref, w_ref, b_ref, lnw_ref, lnb_ref, o_ref, ext_ref, cv_ref, *, width):
    _fill_ext(up_ref, uc_ref, un_ref, ext_ref)
    n_slabs = ext_ref.shape[0]
    d = n_slabs * LANES

    def step(r, carry):
        base = pl.multiple_of(r * CONF_ROWS, CONF_ROWS)
        for s in range(n_slabs):
            cols = slice(s * LANES, (s + 1) * LANES)
            cv_ref[:, cols] = _conv_rows(ext_ref, w_ref, s, base, CONF_ROWS, width) + b_ref[:, cols]
        for r0 in range(0, CONF_ROWS, LN_ROWS):
            u = cv_ref[r0:r0 + LN_ROWS, :]
            xc = u - jnp.mean(u, axis=-1, keepdims=True)
            y = xc * lax.rsqrt(jnp.mean(xc * xc, axis=-1, keepdims=True) + LN_EPS)
            y = y * lnw_ref[...] + lnb_ref[...]
            o_ref[pl.ds(base + r0, LN_ROWS), :] = _silu(y).astype(o_ref.dtype)
        return carry

    lax.fori_loop(0, uc_ref.shape[0] // CONF_ROWS, step, 0)


def conformer_conv_ln(u, dw_w, dw_b, ln_w, ln_b):
    t, d = u.shape
    width = dw_w.shape[0]
    assert width // 2 <= HALO and d % LANES == 0
    tm = _pick(t, (256, 128, 64))
    row = pl.BlockSpec((1, d), lambda i, j: (0, 0))
    return pl.pallas_call(
        functools.partial(_conf_conv_kernel, width=width),
        grid=(t // tm, 1),
        in_specs=_halo_specs(tm, d, t) + [pl.BlockSpec((width, d), lambda i, j: (0, 0)), row, row, row],
        out_specs=pl.BlockSpec((tm, d), lambda i, j: (i, 0)),
        out_shape=jax.ShapeDtypeStruct((t, d), BF16),
        scratch_shapes=[pltpu.VMEM((d // LANES, tm + 2 * HALO, LANES), F32), pltpu.VMEM((CONF_ROWS, d), F32)],
        compiler_params=_cparams(("arbitrary", "arbitrary")),
        name="conformer_conv_ln",
    )(u, u, u, dw_w, dw_b, ln_w, ln_b)


QKV_ROWS = 64


def _qkv_conv_kernel(up_ref, uc_ref, un_ref, w_ref, o_ref, ext_ref, *, width, q_tiles, qk_tiles, q_scale):
    _fill_ext(up_ref, uc_ref, un_ref, ext_ref)
    j = pl.program_id(1)
    scale = jnp.where(j < q_tiles, q_scale, 1.0).astype(F32)

    def make_step(normalise):
        def step(r, carry):
            base = pl.multiple_of(r * QKV_ROWS, QKV_ROWS)
            for s in range(ext_ref.shape[0]):
                y = _silu(_conv_rows(ext_ref, w_ref, s, base, QKV_ROWS, width))
                if normalise:
                    y = y * (lax.rsqrt(jnp.sum(y * y, axis=-1, keepdims=True) + L2_EPS) * scale)
                o_ref[pl.ds(base, QKV_ROWS), s * LANES:(s + 1) * LANES] = y.astype(o_ref.dtype)
            return carry
        return step

    nsteps = uc_ref.shape[0] // QKV_ROWS

    @pl.when(j < qk_tiles)
    def _():
        lax.fori_loop(0, nsteps, make_step(True), 0)

    @pl.when(j >= qk_tiles)
    def _():
        lax.fori_loop(0, nsteps, make_step(False), 0)


def qkv_conv(proj, conv_w, qk_dim):
    t = proj.shape[0]
    width, cdim = conv_w.shape
    assert HEAD_DIM == LANES and width // 2 <= HALO
    tm = _pick(t, (512, 256, 128, 64, 32))
    tc = _pick(qk_dim, (1024, 512, 256, 128))
    return pl.pallas_call(
        functools.partial(_qkv_conv_kernel, width=width, q_tiles=qk_dim // tc,
                          qk_tiles=2 * qk_dim // tc, q_scale=HEAD_DIM ** -0.5),
        grid=(t // tm, cdim // tc),
        in_specs=_halo_specs(tm, tc, t) + [pl.BlockSpec((width, tc), lambda i, j: (0, j))],
        out_specs=pl.BlockSpec((tm, tc), lambda i, j: (i, j)),
        out_shape=jax.ShapeDtypeStruct((t, cdim), BF16),
        scratch_shapes=[pltpu.VMEM((tc // LANES, tm + 2 * HALO, LANES), F32)],
        compiler_params=_cparams(("arbitrary", "arbitrary")),
        name="qkv_conv",
    )(proj, proj, proj, conv_w)


def _split3(x):
    hi = x.astype(BF16)
    r = x - hi.astype(F32)
    mid = r.astype(BF16)
    lo = (r - mid.astype(F32)).astype(BF16)
    return hi, mid, lo


def _gate_kernel(ba_ref, alog_ref, dt_ref, o_ref, *, n_heads):
    tm = ba_ref.shape[0]
    ri = lax.broadcasted_iota(jnp.int32, (CHUNK, CHUNK), 0)
    ci = lax.broadcasted_iota(jnp.int32, (CHUNK, CHUNK), 1)
    tril = jnp.where(ri >= ci, 1.0, 0.0).astype(BF16)
    triu = jnp.where(ri <= ci, 1.0, 0.0).astype(BF16)
    lane = lax.broadcasted_iota(jnp.int32, (CHUNK, 4 * n_heads), 1)
    for c in range(tm // CHUNK):
        ba = ba_ref[c * CHUNK:(c + 1) * CHUNK, :]
        beta = _sigmoid(ba)
        x = ba + dt_ref[...]
        softplus = jnp.maximum(x, 0.0) + jnp.log(1.0 + jnp.exp(-jnp.abs(x)))
        g = -jnp.exp(alog_ref[...]) * softplus
        pre = jnp.zeros_like(g)
        suf = jnp.zeros_like(g)
        for part in _split3(g):
            pre = pre + jnp.dot(tril, part, preferred_element_type=F32)
            suf = suf + jnp.dot(triu, part, preferred_element_type=F32)
        cum = jnp.where(lane < 3 * n_heads, pre, suf)
        o_ref[c * CHUNK:(c + 1) * CHUNK, :] = jnp.where(lane < 2 * n_heads, beta, cum)


def gdn_gates(ba, a_log, dt_bias):
    t, n = ba.shape
    nv = n // 4
    tm = _pick(t, (512, 256, 128, 64))
    zeros = jnp.zeros((2 * nv,), F32)
    alog_row = jnp.concatenate([zeros, a_log.reshape(-1).astype(F32)]).reshape(1, n)
    dt_row = jnp.concatenate([zeros, dt_bias.reshape(-1).astype(F32)]).reshape(1, n)
    row = pl.BlockSpec((1, n), lambda i: (0, 0))
    return pl.pallas_call(
        functools.partial(_gate_kernel, n_heads=nv),
        grid=(t // tm,),
        in_specs=[pl.BlockSpec((tm, n), lambda i: (i, 0)), row, row],
        out_specs=pl.BlockSpec((tm, n), lambda i: (i, 0)),
        out_shape=jax.ShapeDtypeStruct((t, n), F32),
        compiler_params=_cparams(("arbitrary",)),
        name="gdn_gates",
    )(ba, alog_row, dt_row)


def _bdot(a, b, ca=2, cb=1):
    return lax.dot_general(a.astype(BF16), b.astype(BF16), (((ca,), (cb,)), ((0,), (0,))),
                           preferred_element_type=F32)


N_STREAMS = 4


class _PackedMasks:
    def __init__(self):
        c, w = CHUNK, N_STREAMS * CHUNK
        ri = lax.broadcasted_iota(jnp.int32, (c, w), 0)
        li = lax.broadcasted_iota(jnp.int32, (c, w), 1)
        ci = jnp.bitwise_and(li, c - 1)
        fwd = li < (N_STREAMS // 2) * c
        self.eye = ri == ci
        bwd = jnp.logical_not(fwd)
        self.incl = jnp.logical_or(jnp.logical_and(fwd, ri >= ci), jnp.logical_and(bwd, ri <= ci))
        self.strict = jnp.logical_or(jnp.logical_and(fwd, ri > ci), jnp.logical_and(bwd, ri < ci))
        self.seg = [jnp.right_shift(li, 6) == s for s in range(N_STREAMS)]
        self.same16 = jnp.right_shift(ri, 4) == jnp.right_shift(ci, 4)
        self.same32 = jnp.right_shift(ri, 5) == jnp.right_shift(ci, 5)
        rb = lax.broadcasted_iota(jnp.int32, (w, w), 0)
        lb = lax.broadcasted_iota(jnp.int32, (w, w), 1)
        self.blockdiag = jnp.right_shift(rb, 6) == jnp.right_shift(lb, 6)


    def spread(self, cols):
        out = cols[-1]
        for s in range(N_STREAMS - 2, -1, -1):
            out = jnp.where(self.seg[s], cols[s], out)
        return out

    def columns(self, row):
        z = jnp.where(self.eye, row, 0.0)
        return [jnp.sum(jnp.where(self.seg[s], z, 0.0), axis=-1, keepdims=True) for s in range(N_STREAMS)]

    def bd(self, xp):
        return jnp.where(self.blockdiag, jnp.concatenate([xp] * N_STREAMS, axis=-2), 0.0).astype(BF16)


def _inv_unit_triangular_packed(lp, m):
    mm = lambda xp, yp: _bdot(xp, m.bd(yp))
    l16 = jnp.where(m.same16, lp, 0.0)
    c32 = jnp.where(jnp.logical_and(m.same32, jnp.logical_not(m.same16)), lp, 0.0)
    c64 = jnp.where(m.same32, 0.0, lp)
    p = jnp.where(m.eye, 1.0, 0.0) - l16
    sq = mm(l16, l16)
    for _ in range(2):
        both = mm(jnp.concatenate([p, sq], axis=-2), sq)
        p, sq = p + both[:, :CHUNK], both[:, CHUNK:]
    p = p + mm(p, sq)
    y = p - mm(mm(p, c32), p)
    return y - mm(mm(y, c64), y)


def _gdn_prep_kernel(q_ref, k_ref, v_ref, tab_ref, gl_ref, kt_ref, uw_ref, qo_ref, *, chunks):
    c, dk, g = CHUNK, HEAD_DIM, PREP_GROUP
    m = _PackedMasks()

    def group_step(i, carry):
        j0 = pl.multiple_of(i * g, g)
        rows = pl.ds(pl.multiple_of(i * (g * c), g * c), g * c)
        q = q_ref[rows, :].reshape(g, c, dk)
        k = k_ref[rows, :].reshape(g, c, dk)
        v = v_ref[rows, :].reshape(g, c, 2 * dk)
        tab, gl = tab_ref[pl.ds(j0, g)], gl_ref[pl.ds(j0, g)]
        beta_p, g_p = tab[:, 0:1, :], tab[:, 1:2, :]
        x = _bdot(jnp.concatenate([q, k], axis=1), jnp.concatenate([k, k], axis=1), 2, 2)
        qk_p = jnp.concatenate([x[:, :c], x[:, :c]], axis=2)
        kk_p = jnp.concatenate([x[:, c:], x[:, c:]], axis=2)
        g_cols = m.columns(g_p)
        g_cb = m.spread(g_cols)
        beta_cb = m.spread(m.columns(beta_p))
        decay = jnp.exp(jnp.where(m.incl, g_cb - g_p, -jnp.inf))
        l_p = jnp.where(m.strict, kk_p * decay * beta_cb, 0.0)
        a_p = qk_p * decay
        t_p = _inv_unit_triangular_packed(l_p, m)
        k_st = jnp.concatenate([k] * N_STREAMS, axis=1)
        v_st = jnp.concatenate([v[:, :, :dk], v[:, :, dk:]] * (N_STREAMS // 2), axis=1)
        g_cst = jnp.concatenate(g_cols, axis=1)
        eg_cst = jnp.exp(g_cst)
        vk = jnp.concatenate([v_st, (k_st.astype(F32) * eg_cst).astype(BF16)], axis=2)
        uw = _bdot(m.bd(t_p * beta_p), vk).astype(BF16)
        a_uw = _bdot(m.bd(a_p), uw)
        q_st = jnp.concatenate([q] * N_STREAMS, axis=1).astype(F32)
        q_prime = q_st * eg_cst - a_uw[:, :, dk:]
        qo = jnp.concatenate([q_prime, a_uw[:, :, :dk]], axis=2).astype(qo_ref.dtype)
        gl_st = jnp.concatenate([jnp.broadcast_to(gl[:, s:s + 1, :], (g, c, dk)) for s in range(N_STREAMS)], axis=1)
        kt_st = (k_st.astype(F32) * jnp.exp(gl_st - g_cst)).astype(BF16)
        for s in range(N_STREAMS):
            seg = slice(s * c, (s + 1) * c)
            d, vh = divmod(s, 2)
            kt_ref[d, vh, rows, :] = kt_st[:, seg].reshape(g * c, dk)
            uw_ref[d, vh, rows, :] = uw[:, seg].reshape(g * c, 2 * dk)
            qo_ref[d, vh, rows, :] = qo[:, seg].reshape(g * c, 2 * dk)
        return carry

    lax.fori_loop(0, chunks // g, group_step, 0)


SCAN_HEADS = 16


def _gdn_scan_kernel(kt_ref, uw_ref, qo_ref, gl_ref, *rest, chunks, reverse):
    c, dk = CHUNK, HEAD_DIM
    if reverse:
        of_ref, z_ref, nw_ref, out_ref, s_ref = rest
    else:
        out_ref, s_ref = rest

    @pl.when(pl.program_id(1) == 0)
    def _():
        s_ref[...] = jnp.zeros_like(s_ref)

    def chunk_step(j, carry):
        cj = chunks - 1 - j if reverse else j
        rows = pl.ds(pl.multiple_of(cj * c, c), c)
        pr_all = [lax.dot_general(kt_ref[vh, rows, :], uw_ref[vh, rows, :], (((0,), (0,)), ((), ())),
                                  preferred_element_type=F32) for vh in range(SCAN_HEADS)]
        for pair in range(SCAN_HEADS // 2):
            vhs = (2 * pair, 2 * pair + 1)
            prs = [pr_all[vh] for vh in vhs]
            qos = [qo_ref[vh, rows, :] for vh in vhs]
            ss = [s_ref[vh] for vh in vhs]
            lhs = jnp.concatenate([jnp.concatenate([pr[:, dk:].astype(BF16), qo[:, :dk]], axis=0)
                                   for pr, qo in zip(prs, qos)], axis=1)
            zero = jnp.zeros((dk, dk), BF16)
            s_bd = jnp.concatenate([jnp.concatenate([ss[0].astype(BF16), zero], axis=1),
                                    jnp.concatenate([zero, ss[1].astype(BF16)], axis=1)], axis=0)
            x = jnp.dot(lhs, s_bd, preferred_element_type=F32)
            for n, vh in enumerate(vhs):
                xs = x[:, n * dk:(n + 1) * dk]
                cols = slice(vh * dk, (vh + 1) * dk)
                o = xs[dk:] + qos[n][:, dk:].astype(F32)
                if reverse:
                    tot = o + of_ref[rows, cols].astype(F32)
                    o = _rms(tot) * nw_ref[...] * _silu(z_ref[rows, cols].astype(F32))
                out_ref[rows, cols] = o.astype(out_ref.dtype)
                s_ref[vh] = ss[n] * jnp.exp(gl_ref[vh, cj]) + prs[n][:, :dk] - xs[:dk]
        return carry

    lax.fori_loop(0, chunks, chunk_step, 0)


PREP_GROUP = 8
PREP_CHUNKS = 32
SCAN_CHUNKS = 8


def gdn_core(qkv, proj, gates, norm_w, n_qk, n_v):
    t = qkv.shape[0]
    c, dk = CHUNK, HEAD_DIM
    dv = dk
    assert n_v == 2 * n_qk and t % c == 0 and n_v % SCAN_HEADS == 0
    nc = t // c
    qk_dim, v_dim = n_qk * dk, n_v * dv
    beta = gates[:, :2 * n_v].reshape(nc, c, 2, n_qk, 2)
    cum = gates[:, 2 * n_v:].reshape(nc, c, 2, n_qk, 2)
    packed = lambda a: a.transpose(3, 0, 2, 4, 1).reshape(n_qk, nc, 1, N_STREAMS * c)
    tab = jnp.concatenate([packed(beta), packed(cum), jnp.zeros((n_qk, nc, 6, N_STREAMS * c), F32)], axis=2)
    last = jnp.stack([cum[:, c - 1, 0], cum[:, 0, 1]])
    gl_prep = last.transpose(2, 1, 0, 3).reshape(n_qk, nc, N_STREAMS, 1)
    gl_prep = jnp.broadcast_to(jnp.pad(gl_prep, ((0, 0), (0, 0), (0, 8 - N_STREAMS), (0, 0))), (n_qk, nc, 8, dk))
    gl_scan = jnp.broadcast_to(last.reshape(2, nc, n_v).transpose(0, 2, 1)[..., None, None], (2, n_v, nc, 1, dv))

    pc = _pick(nc, (PREP_CHUNKS, PREP_GROUP))
    assert nc % pc == 0 and pc % PREP_GROUP == 0
    rb = pc * c
    per_row = lambda width: pl.BlockSpec((2, 2, rb, width), lambda h, b: (0, h, b, 0))
    kt, uw, qo = pl.pallas_call(
        functools.partial(_gdn_prep_kernel, chunks=pc),
        grid=(n_qk, nc // pc),
        in_specs=[pl.BlockSpec((rb, dk), lambda h, b: (b, h)),
                  pl.BlockSpec((rb, dk), lambda h, b: (b, n_qk + h)),
                  pl.BlockSpec((rb, 2 * dv), lambda h, b: (b, qk_dim // dv + h)),
                  pl.BlockSpec((None, pc, 8, N_STREAMS * c), lambda h, b: (h, b, 0, 0)),
                  pl.BlockSpec((None, pc, 8, dk), lambda h, b: (h, b, 0, 0))],
        out_specs=[per_row(dk), per_row(2 * dv), per_row(2 * dv)],
        out_shape=[jax.ShapeDtypeStruct((2, n_v, t, dk), BF16),
                   jax.ShapeDtypeStruct((2, n_v, t, 2 * dv), BF16),
                   jax.ShapeDtypeStruct((2, n_v, t, 2 * dv), BF16)],
        compiler_params=_cparams(("arbitrary", "arbitrary")),
        name="gdn_prep",
    )(qkv, qkv, qkv, tab, gl_prep)

    sc = _pick(nc, (SCAN_CHUNKS, 4, 2, 1))
    ng = nc // sc
    rs = sc * c
    wide = SCAN_HEADS * dv
    z_off = (2 * qk_dim + v_dim) // wide

    def scan(reverse, extra_args, extra_specs):
        d = int(reverse)
        blk = (lambda b: ng - 1 - b) if reverse else (lambda b: b)
        tile = pl.BlockSpec((rs, wide), lambda g, b: (blk(b), g))
        per_row = lambda width: pl.BlockSpec((None, SCAN_HEADS, rs, width), lambda g, b: (d, g, blk(b), 0))
        return pl.pallas_call(
            functools.partial(_gdn_scan_kernel, chunks=sc, reverse=reverse),
            grid=(n_v // SCAN_HEADS, ng),
            in_specs=[per_row(dk), per_row(2 * dv), per_row(2 * dv),
                      pl.BlockSpec((None, SCAN_HEADS, sc, 1, dv), lambda g, b: (d, g, blk(b), 0, 0))]
                     + [tile if spec is None else spec(blk) for spec in extra_specs],
            out_specs=tile,
            out_shape=jax.ShapeDtypeStruct((t, v_dim), BF16),
            scratch_shapes=[pltpu.VMEM((SCAN_HEADS, dk, dv), F32)],
            compiler_params=_cparams(("arbitrary", "arbitrary")),
            name="gdn_scan_bwd" if reverse else "gdn_scan_fwd",
        )(kt, uw, qo, gl_scan, *extra_args)

    o_f = scan(False, (), ())
    z_spec = lambda blk: pl.BlockSpec((rs, wide), lambda g, b: (blk(b), z_off + g))
    nw_spec = lambda blk: pl.BlockSpec((1, dv), lambda g, b: (0, 0))
    return scan(True, (o_f, proj, norm_w.reshape(1, dv).astype(F32)), (None, z_spec, nw_spec))


def kernel(x, c, ada_w, ada_b, norm_w, gdn_in_w, gdn_conv_w, gdn_A_log, gdn_dt_bias, gdn_norm_w, gdn_out_w, cf_pw1_w, cf_pw1_b, cf_dw_w, cf_dw_b, cf_ln_w, cf_ln_b, cf_pw2_w, cf_pw2_b, ffn_in_w, ffn_out_w):
    bsz, t, d = x.shape
    assert bsz == 1
    depth = ada_w.shape[0]
    n_v = gdn_A_log.shape[-1]
    n_qk = n_v // 2
    qk_dim, v_dim = n_qk * HEAD_DIM, n_v * HEAD_DIM
    conv_dim = 2 * qk_dim + v_dim
    row = lambda a: a.reshape(1, -1).astype(F32)
    zero_row = jnp.zeros((1, d), F32)

    mod = ada_modulation(c, ada_w, ada_b)[:, 0].reshape(depth, 6, d)
    xs = x[0]
    h = prenorm(xs, row(norm_w[0, 0]), row(mod[0, 1]), row(mod[0, 0]))
    for i in range(depth):
        j = i // N_MIXERS
        if i % N_MIXERS == 0:
            proj = matmul(h, gdn_in_w, j, 0, conv_dim + v_dim, BF16)
            ba = matmul(h, gdn_in_w, j, conv_dim + v_dim, 4 * n_v, F32)
            gates = gdn_gates(ba, gdn_A_log[j], gdn_dt_bias[j])
            qkv = qkv_conv(proj, gdn_conv_w[j].astype(F32), qk_dim)
            mix = gdn_core(qkv, proj, gates, gdn_norm_w[j], n_qk, n_v)
            w_out, b_out = gdn_out_w[j].astype(BF16), zero_row
        else:
            u = matmul_glu(h, cf_pw1_w, j, row(cf_pw1_b[j]), "glu")
            mix = conformer_conv_ln(u, cf_dw_w[j].astype(F32), row(cf_dw_b[j]), row(cf_ln_w[j]), row(cf_ln_b[j]))
            w_out, b_out = cf_pw2_w[j].astype(BF16), row(cf_pw2_b[j])
        xs, h = matmul_residual(mix, w_out, b_out, xs, row(norm_w[i, 1]), row(mod[i, 2]),
                                row(norm_w[i, 2]), row(mod[i, 4]), row(mod[i, 3]), True)
        f = matmul_glu(h, ffn_in_w, i, jnp.zeros((1, ffn_in_w.shape[2]), F32), "swiglu")
        last = i == depth - 1
        nxt = (zero_row, zero_row, zero_row) if last else (
            row(norm_w[i + 1, 0]), row(mod[i + 1, 1]), row(mod[i + 1, 0]))
        xs, h = matmul_residual(f, ffn_out_w[i].astype(BF16), zero_row, xs, row(norm_w[i, 3]),
                                row(mod[i, 5]), *nxt, not last)
    return xs[None]
```

```python
import functools

import jax
import jax.numpy as jnp
from jax import lax
from jax.experimental import pallas as pl
from jax.experimental.pallas import tpu as pltpu

F32 = jnp.float32
BF16 = jnp.bfloat16

HEAD_DIM = 128
N_MIXERS = 2
CHUNK = 64
RMS_EPS = 1e-6
LN_EPS = 1e-5
L2_EPS = 1e-6

V7X_VMEM_BYTES = 64 * 1024 * 1024
VMEM_LIMIT = V7X_VMEM_BYTES - 12 * 1024 * 1024
HALO = 16


def _cparams(sem):
    return pltpu.CompilerParams(dimension_semantics=sem, vmem_limit_bytes=VMEM_LIMIT)


def _pick(n, prefs):
    for p in prefs:
        if n % p == 0:
            return p
    return n


def _sigmoid(x):
    return 1.0 / (1.0 + jnp.exp(-x))


def _silu(x):
    return x * _sigmoid(x)


def _rms(y):
    return y * lax.rsqrt(jnp.mean(y * y, axis=-1, keepdims=True) + RMS_EPS)


def _ada_kernel(c_ref, w_ref, b_ref, o_ref):
    cond = _silu(c_ref[...])
    o_ref[...] = jnp.dot(cond.astype(BF16), w_ref[...].astype(BF16),
                         preferred_element_type=F32) + b_ref[...]


def ada_modulation(c, ada_w, ada_b):
    depth, d, n = ada_w.shape
    b = c.shape[0]
    assert b <= 8
    cp = jnp.zeros((8, d), F32).at[:b].set(c)
    tn = _pick(n, (1024, 512, 256, 128))
    out = pl.pallas_call(
        _ada_kernel,
        grid=(depth, n // tn),
        in_specs=[pl.BlockSpec((8, d), lambda l, j: (0, 0)),
                  pl.BlockSpec((None, d, tn), lambda l, j: (l, 0, j)),
                  pl.BlockSpec((None, 1, tn), lambda l, j: (l, 0, j))],
        out_specs=pl.BlockSpec((None, 8, tn), lambda l, j: (l, 0, j)),
        out_shape=jax.ShapeDtypeStruct((depth, 8, n), F32),
        compiler_params=_cparams(("arbitrary", "arbitrary")),
        name="ada_modulation",
    )(cp, ada_w, ada_b.reshape(depth, 1, n))
    return out[:, :b]


def _prenorm_kernel(x_ref, nw_ref, sc_ref, sh_ref, h_ref):
    y = _rms(x_ref[...])
    h_ref[...] = (y * nw_ref[...] * (1.0 + sc_ref[...]) + sh_ref[...]).astype(h_ref.dtype)


def prenorm(x, nw, sc, sh):
    t, d = x.shape
    tm = _pick(t, (512, 256, 128, 64, 8))
    row = pl.BlockSpec((1, d), lambda i: (0, 0))
    return pl.pallas_call(
        _prenorm_kernel,
        grid=(t // tm,),
        in_specs=[pl.BlockSpec((tm, d), lambda i: (i, 0)), row, row, row],
        out_specs=pl.BlockSpec((tm, d), lambda i: (i, 0)),
        out_shape=jax.ShapeDtypeStruct((t, d), BF16),
        compiler_params=_cparams(("arbitrary",)),
        name="prenorm",
    )(x, nw, sc, sh)


def _mm_kernel(x_ref, w_ref, o_ref, wb_ref):
    @pl.when(pl.program_id(1) == 0)
    def _():
        wb_ref[...] = w_ref[...].astype(BF16)

    o_ref[...] = jnp.dot(x_ref[...], wb_ref[...], preferred_element_type=F32).astype(o_ref.dtype)


def matmul(x, w, layer, col0, n, out_dtype, tm_prefs=(1024, 512, 256, 128, 64), tn_prefs=(1024, 512, 256, 128)):
    m, k = x.shape
    tm, tn = _pick(m, tm_prefs), _pick(n, tn_prefs)
    assert col0 % tn == 0
    j0 = col0 // tn
    return pl.pallas_call(
        _mm_kernel,
        grid=(n // tn, m // tm),
        in_specs=[pl.BlockSpec((tm, k), lambda j, i: (i, 0)),
                  pl.BlockSpec((None, k, tn), lambda j, i: (layer, 0, j0 + j))],
        out_specs=pl.BlockSpec((tm, tn), lambda j, i: (i, j)),
        out_shape=jax.ShapeDtypeStruct((m, n), out_dtype),
        scratch_shapes=[pltpu.VMEM((k, tn), BF16)],
        compiler_params=_cparams(("arbitrary", "arbitrary")),
        name="matmul",
    )(x, w)


def _mm_glu_kernel(x_ref, wa_ref, wb_ref, ba_ref, bb_ref, o_ref, wab_ref, wbb_ref, *, act):
    @pl.when(pl.program_id(1) == 0)
    def _():
        wab_ref[...] = wa_ref[...].astype(BF16)
        wbb_ref[...] = wb_ref[...].astype(BF16)

    x = x_ref[...]
    a = jnp.dot(x, wab_ref[...], preferred_element_type=F32) + ba_ref[...]
    b = jnp.dot(x, wbb_ref[...], preferred_element_type=F32) + bb_ref[...]
    if act == "swiglu":
        r = _silu(a) * b
    else:
        r = a * _sigmoid(b)
    o_ref[...] = r.astype(o_ref.dtype)


def matmul_glu(x, w, layer, bias, act, tm_prefs=(1024, 512, 256, 128, 64), tn_prefs=(512, 256, 128)):
    m, k = x.shape
    n = w.shape[2] // 2
    tm, tn = _pick(m, tm_prefs), _pick(n, tn_prefs)
    nb = n // tn
    return pl.pallas_call(
        functools.partial(_mm_glu_kernel, act=act),
        grid=(nb, m // tm),
        in_specs=[pl.BlockSpec((tm, k), lambda j, i: (i, 0)),
                  pl.BlockSpec((None, k, tn), lambda j, i: (layer, 0, j)),
                  pl.BlockSpec((None, k, tn), lambda j, i: (layer, 0, j + nb)),
                  pl.BlockSpec((1, tn), lambda j, i: (0, j)),
                  pl.BlockSpec((1, tn), lambda j, i: (0, j + nb))],
        out_specs=pl.BlockSpec((tm, tn), lambda j, i: (i, j)),
        out_shape=jax.ShapeDtypeStruct((m, n), BF16),
        scratch_shapes=[pltpu.VMEM((k, tn), BF16), pltpu.VMEM((k, tn), BF16)],
        compiler_params=_cparams(("arbitrary", "arbitrary")),
        name="matmul_" + act,
    )(x, w, w, bias, bias)


EPI_ROWS = 32


def _mm_res_kernel(a_ref, w_ref, b_ref, x_ref, nwp_ref, g_ref, nwn_ref, sc_ref, sh_ref,
                   xo_ref, *rest, nn):
    *maybe_ho_ref, y_even, y_odd = rest
    i, j = pl.program_id(0), pl.program_id(1)
    n_tiles = pl.num_programs(0) - 1
    tm, tn = y_even.shape[1], y_even.shape[2]
    d = nn * tn
    rows_per_step = tm // nn
    col = lambda ref, jj: ref[:, jj * tn:(jj + 1) * tn]

    @pl.when(jnp.logical_and(i == 0, j == 0))
    def _():
        y_odd[...] = jnp.zeros_like(y_odd)

    def epilogue_slice(y_prev):
        for r in range(rows_per_step // EPI_ROWS):
            rows = pl.ds(r * EPI_ROWS, EPI_ROWS)
            yrows = pl.ds(pl.multiple_of(j * rows_per_step + r * EPI_ROWS, EPI_ROWS), EPI_ROWS)
            ys = [y_prev[jj, yrows, :] + col(b_ref, jj) for jj in range(nn)]
            inv = lax.rsqrt(sum(jnp.sum(y * y, axis=-1, keepdims=True) for y in ys) / d + RMS_EPS)
            xns = [x_ref[rows, jj * tn:(jj + 1) * tn] + (1.0 + col(g_ref, jj)) * (ys[jj] * inv * col(nwp_ref, jj))
                   for jj in range(nn)]
            for jj in range(nn):
                xo_ref[rows, jj * tn:(jj + 1) * tn] = xns[jj]
            if maybe_ho_ref:
                inv2 = lax.rsqrt(sum(jnp.sum(v * v, axis=-1, keepdims=True) for v in xns) / d + RMS_EPS)
                for jj in range(nn):
                    hn = xns[jj] * inv2 * col(nwn_ref, jj) * (1.0 + col(sc_ref, jj)) + col(sh_ref, jj)
                    maybe_ho_ref[0][rows, jj * tn:(jj + 1) * tn] = hn.astype(BF16)

    def step(y_cur, y_prev):
        y_cur[j] = jnp.dot(a_ref[...], w_ref[...], preferred_element_type=F32)
        epilogue_slice(y_prev)

    even, busy = lax.rem(i, 2) == 0, i < n_tiles
    pl.when(jnp.logical_and(busy, even))(lambda: step(y_even, y_odd))
    pl.when(jnp.logical_and(busy, jnp.logical_not(even)))(lambda: step(y_odd, y_even))
    pl.when(jnp.logical_and(jnp.logical_not(busy), even))(lambda: epilogue_slice(y_odd))
    pl.when(jnp.logical_and(jnp.logical_not(busy), jnp.logical_not(even)))(lambda: epilogue_slice(y_even))


def matmul_residual(a, w, bias, x, nw_post, gate, nw_next, sc_next, sh_next, emit_next,
                    tm_prefs=(512, 256, 128, 64)):
    m, kdim = a.shape
    d = w.shape[1]
    tm = _pick(m, tm_prefs)
    tn = _pick(d, (1024, 512, 256, 128) if kdim <= 4096 else (512, 256, 128))
    nn = d // tn
    n_tiles = m // tm
    assert (tm // nn) % EPI_ROWS == 0
    row = pl.BlockSpec((1, d), lambda i, j: (0, 0))
    tile = pl.BlockSpec((tm // nn, d), lambda i, j: (jnp.maximum((i - 1) * nn + j, 0), 0))
    out_shape = [jax.ShapeDtypeStruct((m, d), F32)]
    out_specs = [tile]
    if emit_next:
        out_shape.append(jax.ShapeDtypeStruct((m, d), BF16))
        out_specs.append(tile)
    res = pl.pallas_call(
        functools.partial(_mm_res_kernel, nn=nn),
        grid=(n_tiles + 1, nn),
        in_specs=[pl.BlockSpec((tm, kdim), lambda i, j: (jnp.minimum(i, n_tiles - 1), 0)),
                  pl.BlockSpec((kdim, tn), lambda i, j: (0, j)),
                  row, tile, row, row, row, row, row],
        out_specs=out_specs,
        out_shape=out_shape,
        scratch_shapes=[pltpu.VMEM((nn, tm, tn), F32), pltpu.VMEM((nn, tm, tn), F32)],
        compiler_params=_cparams(("arbitrary", "arbitrary")),
        name="matmul_residual",
    )(a, w, bias, x, nw_post, gate, nw_next, sc_next, sh_next)
    return (res[0], res[1]) if emit_next else (res[0], None)


LANES = 128


def _halo_specs(tm, tc, t):
    per = tm // HALO
    last = t // HALO - 1
    return [pl.BlockSpec((HALO, tc), lambda i, j: (jnp.maximum(i * per - 1, 0), j)),
            pl.BlockSpec((tm, tc), lambda i, j: (i, j)),
            pl.BlockSpec((HALO, tc), lambda i, j: (jnp.minimum((i + 1) * per, last), j))]


def _fill_ext(up_ref, uc_ref, un_ref, ext_ref):
    i, n = pl.program_id(0), pl.num_programs(0)
    tm = uc_ref.shape[0]
    for s in range(ext_ref.shape[0]):
        cols = slice(s * LANES, (s + 1) * LANES)
        ext_ref[s, 0:HALO, :] = jnp.where(i > 0, up_ref[:, cols].astype(F32), 0.0)
        ext_ref[s, HALO:HALO + tm, :] = uc_ref[:, cols].astype(F32)
        ext_ref[s, HALO + tm:HALO + tm + HALO, :] = jnp.where(i < n - 1, un_ref[:, cols].astype(F32), 0.0)


def _conv_rows(ext_ref, w_ref, slab, base, rows, width):
    pad = width // 2
    cols = slice(slab * LANES, (slab + 1) * LANES)
    acc = None
    for j in range(width):
        term = ext_ref[slab, pl.ds(base + (HALO - pad + j), rows), :] * w_ref[j:j + 1, cols]
        acc = term if acc is None else acc + term
    return acc


CONF_ROWS = 64
LN_ROWS = 16


def _conf_conv_kernel(up_ref, uc_ref, un_ref, w_ref, b_ref, lnw_ref, lnb_ref, o_ref, ext_ref, cv_ref, *, width):
    _fill_ext(up_ref, uc_ref, un_ref, ext_ref)
    n_slabs = ext_ref.shape[0]
    d = n_slabs * LANES

    def step(r, carry):
        base = pl.multiple_of(r * CONF_ROWS, CONF_ROWS)
        for s in range(n_slabs):
            cols = slice(s * LANES, (s + 1) * LANES)
            cv_ref[:, cols] = _conv_rows(ext_ref, w_ref, s, base, CONF_ROWS, width) + b_ref[:, cols]
        for r0 in range(0, CONF_ROWS, LN_ROWS):
            u = cv_ref[r0:r0 + LN_ROWS, :]
            xc = u - jnp.mean(u, axis=-1, keepdims=True)
            y = xc * lax.rsqrt(jnp.mean(xc * xc, axis=-1, keepdims=True) + LN_EPS)
            y = y * lnw_ref[...] + lnb_ref[...]
            o_ref[pl.ds(base + r0, LN_ROWS), :] = _silu(y).astype(o_ref.dtype)
        return carry

    lax.fori_loop(0, uc_ref.shape[0] // CONF_ROWS, step, 0)


def conformer_conv_ln(u, dw_w, dw_b, ln_w, ln_b):
    t, d = u.shape
    width = dw_w.shape[0]
    assert width // 2 <= HALO and d % LANES == 0
    tm = _pick(t, (256, 128, 64))
    row = pl.BlockSpec((1, d), lambda i, j: (0, 0))
    return pl.pallas_call(
        functools.partial(_conf_conv_kernel, width=width),
        grid=(t // tm, 1),
        in_specs=_halo_specs(tm, d, t) + [pl.BlockSpec((width, d), lambda i, j: (0, 0)), row, row, row],
        out_specs=pl.BlockSpec((tm, d), lambda i, j: (i, 0)),
        out_shape=jax.ShapeDtypeStruct((t, d), BF16),
        scratch_shapes=[pltpu.VMEM((d // LANES, tm + 2 * HALO, LANES), F32), pltpu.VMEM((CONF_ROWS, d), F32)],
        compiler_params=_cparams(("arbitrary", "arbitrary")),
        name="conformer_conv_ln",
    )(u, u, u, dw_w, dw_b, ln_w, ln_b)


QKV_ROWS = 64


def _qkv_conv_kernel(up_ref, uc_ref, un_ref, w_ref, o_ref, ext_ref, *, width, q_tiles, qk_tiles, q_scale):
    _fill_ext(up_ref, uc_ref, un_ref, ext_ref)
    j = pl.program_id(1)
    scale = jnp.where(j < q_tiles, q_scale, 1.0).astype(F32)

    def make_step(normalise):
        def step(r, carry):
            base = pl.multiple_of(r * QKV_ROWS, QKV_ROWS)
            for s in range(ext_ref.shape[0]):
                y = _silu(_conv_rows(ext_ref, w_ref, s, base, QKV_ROWS, width))
                if normalise:
                    y = y * (lax.rsqrt(jnp.sum(y * y, axis=-1, keepdims=True) + L2_EPS) * scale)
                o_ref[pl.ds(base, QKV_ROWS), s * LANES:(s + 1) * LANES] = y.astype(o_ref.dtype)
            return carry
        return step

    nsteps = uc_ref.shape[0] // QKV_ROWS

    @pl.when(j < qk_tiles)
    def _():
        lax.fori_loop(0, nsteps, make_step(True), 0)

    @pl.when(j >= qk_tiles)
    def _():
        lax.fori_loop(0, nsteps, make_step(False), 0)


def qkv_conv(proj, conv_w, qk_dim):
    t = proj.shape[0]
    width, cdim = conv_w.shape
    assert HEAD_DIM == LANES and width // 2 <= HALO
    tm = _pick(t, (512, 256, 128, 64, 32))
    tc = _pick(qk_dim, (2048, 1024, 512, 256, 128))
    return pl.pallas_call(
        functools.partial(_qkv_conv_kernel, width=width, q_tiles=qk_dim // tc,
                          qk_tiles=2 * qk_dim // tc, q_scale=HEAD_DIM ** -0.5),
        grid=(t // tm, cdim // tc),
        in_specs=_halo_specs(tm, tc, t) + [pl.BlockSpec((width, tc), lambda i, j: (0, j))],
        out_specs=pl.BlockSpec((tm, tc), lambda i, j: (i, j)),
        out_shape=jax.ShapeDtypeStruct((t, cdim), BF16),
        scratch_shapes=[pltpu.VMEM((tc // LANES, tm + 2 * HALO, LANES), F32)],
        compiler_params=_cparams(("arbitrary", "arbitrary")),
        name="qkv_conv",
    )(proj, proj, proj, conv_w)


def _split3(x):
    hi = x.astype(BF16)
    r = x - hi.astype(F32)
    mid = r.astype(BF16)
    lo = (r - mid.astype(F32)).astype(BF16)
    return hi, mid, lo


def _gate_kernel(ba_ref, alog_ref, dt_ref, o_ref, *, n_heads):
    tm = ba_ref.shape[0]
    ri = lax.broadcasted_iota(jnp.int32, (CHUNK, CHUNK), 0)
    ci = lax.broadcasted_iota(jnp.int32, (CHUNK, CHUNK), 1)
    tril = jnp.where(ri >= ci, 1.0, 0.0).astype(BF16)
    triu = jnp.where(ri <= ci, 1.0, 0.0).astype(BF16)
    lane = lax.broadcasted_iota(jnp.int32, (CHUNK, 4 * n_heads), 1)
    for c in range(tm // CHUNK):
        ba = ba_ref[c * CHUNK:(c + 1) * CHUNK, :]
        beta = _sigmoid(ba)
        x = ba + dt_ref[...]
        softplus = jnp.maximum(x, 0.0) + jnp.log(1.0 + jnp.exp(-jnp.abs(x)))
        g = -jnp.exp(alog_ref[...]) * softplus
        pre = jnp.zeros_like(g)
        suf = jnp.zeros_like(g)
        for part in _split3(g):
            pre = pre + jnp.dot(tril, part, preferred_element_type=F32)
            suf = suf + jnp.dot(triu, part, preferred_element_type=F32)
        cum = jnp.where(lane < 3 * n_heads, pre, suf)
        o_ref[c * CHUNK:(c + 1) * CHUNK, :] = jnp.where(lane < 2 * n_heads, beta, cum)


def gdn_gates(ba, a_log, dt_bias):
    t, n = ba.shape
    nv = n // 4
    tm = _pick(t, (512, 256, 128, 64))
    zeros = jnp.zeros((2 * nv,), F32)
    alog_row = jnp.concatenate([zeros, a_log.reshape(-1).astype(F32)]).reshape(1, n)
    dt_row = jnp.concatenate([zeros, dt_bias.reshape(-1).astype(F32)]).reshape(1, n)
    row = pl.BlockSpec((1, n), lambda i: (0, 0))
    return pl.pallas_call(
        functools.partial(_gate_kernel, n_heads=nv),
        grid=(t // tm,),
        in_specs=[pl.BlockSpec((tm, n), lambda i: (i, 0)), row, row],
        out_specs=pl.BlockSpec((tm, n), lambda i: (i, 0)),
        out_shape=jax.ShapeDtypeStruct((t, n), F32),
        compiler_params=_cparams(("arbitrary",)),
        name="gdn_gates",
    )(ba, alog_row, dt_row)


def _bdot(a, b, ca=2, cb=1):
    return lax.dot_general(a.astype(BF16), b.astype(BF16), (((ca,), (cb,)), ((0,), (0,))),
                           preferred_element_type=F32)


N_STREAMS = 4


class _PackedMasks:
    def __init__(self):
        c, w = CHUNK, N_STREAMS * CHUNK
        ri = lax.broadcasted_iota(jnp.int32, (c, w), 0)
        li = lax.broadcasted_iota(jnp.int32, (c, w), 1)
        ci = jnp.bitwise_and(li, c - 1)
        fwd = li < (N_STREAMS // 2) * c
        self.eye = ri == ci
        bwd = jnp.logical_not(fwd)
        self.incl = jnp.logical_or(jnp.logical_and(fwd, ri >= ci), jnp.logical_and(bwd, ri <= ci))
        self.strict = jnp.logical_or(jnp.logical_and(fwd, ri > ci), jnp.logical_and(bwd, ri < ci))
        self.seg = [jnp.right_shift(li, 6) == s for s in range(N_STREAMS)]
        self.same16 = jnp.right_shift(ri, 4) == jnp.right_shift(ci, 4)
        self.same32 = jnp.right_shift(ri, 5) == jnp.right_shift(ci, 5)
        rb = lax.broadcasted_iota(jnp.int32, (w, w), 0)
        lb = lax.broadcasted_iota(jnp.int32, (w, w), 1)
        self.blockdiag = jnp.right_shift(rb, 6) == jnp.right_shift(lb, 6)


    def spread(self, cols):
        out = cols[-1]
        for s in range(N_STREAMS - 2, -1, -1):
            out = jnp.where(self.seg[s], cols[s], out)
        return out

    def columns(self, row):
        z = jnp.where(self.eye, row, 0.0)
        return [jnp.sum(jnp.where(self.seg[s], z, 0.0), axis=-1, keepdims=True) for s in range(N_STREAMS)]

    def bd(self, xp):
        return jnp.where(self.blockdiag, jnp.concatenate([xp] * N_STREAMS, axis=-2), 0.0).astype(BF16)


def _inv_unit_triangular_packed(lp, m):
    mm = lambda xp, yp: _bdot(xp, m.bd(yp))
    l16 = jnp.where(m.same16, lp, 0.0)
    c32 = jnp.where(jnp.logical_and(m.same32, jnp.logical_not(m.same16)), lp, 0.0)
    c64 = jnp.where(m.same32, 0.0, lp)
    p = jnp.where(m.eye, 1.0, 0.0) - l16
    sq = mm(l16, l16)
    for _ in range(2):
        both = mm(jnp.concatenate([p, sq], axis=-2), sq)
        p, sq = p + both[:, :CHUNK], both[:, CHUNK:]
    p = p + mm(p, sq)
    y = p - mm(mm(p, c32), p)
    return y - mm(mm(y, c64), y)


def _gdn_prep_kernel(q_ref, k_ref, v_ref, tab_ref, gl_ref, kt_ref, uw_ref, qo_ref, *, chunks):
    c, dk, g = CHUNK, HEAD_DIM, PREP_GROUP
    m = _PackedMasks()

    def group_step(i, carry):
        j0 = pl.multiple_of(i * g, g)
        rows = pl.ds(pl.multiple_of(i * (g * c), g * c), g * c)
        q = q_ref[rows, :].reshape(g, c, dk)
        k = k_ref[rows, :].reshape(g, c, dk)
        v = v_ref[rows, :].reshape(g, c, 2 * dk)
        tab, gl = tab_ref[pl.ds(j0, g)], gl_ref[pl.ds(j0, g)]
        beta_p, g_p = tab[:, 0:1, :], tab[:, 1:2, :]
        x = _bdot(jnp.concatenate([q, k], axis=1), jnp.concatenate([k, k], axis=1), 2, 2)
        qk_p = jnp.concatenate([x[:, :c], x[:, :c]], axis=2)
        kk_p = jnp.concatenate([x[:, c:], x[:, c:]], axis=2)
        g_cols = m.columns(g_p)
        g_cb = m.spread(g_cols)
        beta_cb = m.spread(m.columns(beta_p))
        decay = jnp.exp(jnp.where(m.incl, g_cb - g_p, -jnp.inf))
        l_p = jnp.where(m.strict, kk_p * decay * beta_cb, 0.0)
        a_p = qk_p * decay
        t_p = _inv_unit_triangular_packed(l_p, m)
        k_st = jnp.concatenate([k] * N_STREAMS, axis=1)
        v_st = jnp.concatenate([v[:, :, :dk], v[:, :, dk:]] * (N_STREAMS // 2), axis=1)
        g_cst = jnp.concatenate(g_cols, axis=1)
        eg_cst = jnp.exp(g_cst)
        vk = jnp.concatenate([v_st, (k_st.astype(F32) * eg_cst).astype(BF16)], axis=2)
        uw = _bdot(m.bd(t_p * beta_p), vk).astype(BF16)
        a_uw = _bdot(m.bd(a_p), uw)
        q_st = jnp.concatenate([q] * N_STREAMS, axis=1).astype(F32)
        q_prime = q_st * eg_cst - a_uw[:, :, dk:]
        qo = jnp.concatenate([q_prime, a_uw[:, :, :dk]], axis=2).astype(qo_ref.dtype)
        gl_st = jnp.concatenate([jnp.broadcast_to(gl[:, s:s + 1, :], (g, c, dk)) for s in range(N_STREAMS)], axis=1)
        kt_st = (k_st.astype(F32) * jnp.exp(gl_st - g_cst)).astype(BF16)
        for s in range(N_STREAMS):
            seg = slice(s * c, (s + 1) * c)
            d, vh = divmod(s, 2)
            kt_ref[d, vh, rows, :] = kt_st[:, seg].reshape(g * c, dk)
            uw_ref[d, vh, rows, :] = uw[:, seg].reshape(g * c, 2 * dk)
            qo_ref[d, vh, rows, :] = qo[:, seg].reshape(g * c, 2 * dk)
        return carry

    lax.fori_loop(0, chunks // g, group_step, 0)


SCAN_HEADS = 16


def _gdn_scan_kernel(kt_ref, uw_ref, qo_ref, gl_ref, *rest, chunks, reverse):
    c, dk = CHUNK, HEAD_DIM
    if reverse:
        of_ref, z_ref, nw_ref, out_ref, s_ref = rest
    else:
        out_ref, s_ref = rest

    @pl.when(pl.program_id(1) == 0)
    def _():
        s_ref[...] = jnp.zeros_like(s_ref)

    def chunk_step(j, carry):
        cj = chunks - 1 - j if reverse else j
        rows = pl.ds(pl.multiple_of(cj * c, c), c)
        pr_all = [lax.dot_general(kt_ref[vh, rows, :], uw_ref[vh, rows, :], (((0,), (0,)), ((), ())),
                                  preferred_element_type=F32) for vh in range(SCAN_HEADS)]
        for pair in range(SCAN_HEADS // 2):
            vhs = (2 * pair, 2 * pair + 1)
            prs = [pr_all[vh] for vh in vhs]
            qos = [qo_ref[vh, rows, :] for vh in vhs]
            ss = [s_ref[vh] for vh in vhs]
            lhs = jnp.concatenate([jnp.concatenate([pr[:, dk:].astype(BF16), qo[:, :dk]], axis=0)
                                   for pr, qo in zip(prs, qos)], axis=1)
            zero = jnp.zeros((dk, dk), BF16)
            s_bd = jnp.concatenate([jnp.concatenate([ss[0].astype(BF16), zero], axis=1),
                                    jnp.concatenate([zero, ss[1].astype(BF16)], axis=1)], axis=0)
            x = jnp.dot(lhs, s_bd, preferred_element_type=F32)
            for n, vh in enumerate(vhs):
                xs = x[:, n * dk:(n + 1) * dk]
                cols = slice(vh * dk, (vh + 1) * dk)
                o = xs[dk:] + qos[n][:, dk:].astype(F32)
                if reverse:
                    tot = o + of_ref[rows, cols].astype(F32)
                    o = _rms(tot) * nw_ref[...] * _silu(z_ref[rows, cols].astype(F32))
                out_ref[rows, cols] = o.astype(out_ref.dtype)
                s_ref[vh] = ss[n] * jnp.exp(gl_ref[vh, cj]) + prs[n][:, :dk] - xs[:dk]
        return carry

    lax.fori_loop(0, chunks, chunk_step, 0)


PREP_GROUP = 8
PREP_CHUNKS = 32
SCAN_CHUNKS = 8


def gdn_core(qkv, proj, gates, norm_w, n_qk, n_v):
    t = qkv.shape[0]
    c, dk = CHUNK, HEAD_DIM
    dv = dk
    assert n_v == 2 * n_qk and t % c == 0 and n_v % SCAN_HEADS == 0
    nc = t // c
    qk_dim, v_dim = n_qk * dk, n_v * dv
    beta = gates[:, :2 * n_v].reshape(nc, c, 2, n_qk, 2)
    cum = gates[:, 2 * n_v:].reshape(nc, c, 2, n_qk, 2)
    packed = lambda a: a.transpose(3, 0, 2, 4, 1).reshape(n_qk, nc, 1, N_STREAMS * c)
    tab = jnp.concatenate([packed(beta), packed(cum), jnp.zeros((n_qk, nc, 6, N_STREAMS * c), F32)], axis=2)
    last = jnp.stack([cum[:, c - 1, 0], cum[:, 0, 1]])
    gl_prep = last.transpose(2, 1, 0, 3).reshape(n_qk, nc, N_STREAMS, 1)
    gl_prep = jnp.broadcast_to(jnp.pad(gl_prep, ((0, 0), (0, 0), (0, 8 - N_STREAMS), (0, 0))), (n_qk, nc, 8, dk))
    gl_scan = jnp.broadcast_to(last.reshape(2, nc, n_v).transpose(0, 2, 1)[..., None, None], (2, n_v, nc, 1, dv))

    pc = _pick(nc, (PREP_CHUNKS, PREP_GROUP))
    assert nc % pc == 0 and pc % PREP_GROUP == 0
    rb = pc * c
    per_row = lambda width: pl.BlockSpec((2, 2, rb, width), lambda h, b: (0, h, b, 0))
    kt, uw, qo = pl.pallas_call(
        functools.partial(_gdn_prep_kernel, chunks=pc),
        grid=(n_qk, nc // pc),
        in_specs=[pl.BlockSpec((rb, dk), lambda h, b: (b, h)),
                  pl.BlockSpec((rb, dk), lambda h, b: (b, n_qk + h)),
                  pl.BlockSpec((rb, 2 * dv), lambda h, b: (b, qk_dim // dv + h)),
                  pl.BlockSpec((None, pc, 8, N_STREAMS * c), lambda h, b: (h, b, 0, 0)),
                  pl.BlockSpec((None, pc, 8, dk), lambda h, b: (h, b, 0, 0))],
        out_specs=[per_row(dk), per_row(2 * dv), per_row(2 * dv)],
        out_shape=[jax.ShapeDtypeStruct((2, n_v, t, dk), BF16),
                   jax.ShapeDtypeStruct((2, n_v, t, 2 * dv), BF16),
                   jax.ShapeDtypeStruct((2, n_v, t, 2 * dv), BF16)],
        compiler_params=_cparams(("arbitrary", "arbitrary")),
        name="gdn_prep",
    )(qkv, qkv, qkv, tab, gl_prep)

    sc = _pick(nc, (SCAN_CHUNKS, 4, 2, 1))
    ng = nc // sc
    rs = sc * c
    wide = SCAN_HEADS * dv
    z_off = (2 * qk_dim + v_dim) // wide

    def scan(reverse, extra_args, extra_specs):
        d = int(reverse)
        blk = (lambda b: ng - 1 - b) if reverse else (lambda b: b)
        tile = pl.BlockSpec((rs, wide), lambda g, b: (blk(b), g))
        per_row = lambda width: pl.BlockSpec((None, SCAN_HEADS, rs, width), lambda g, b: (d, g, blk(b), 0))
        return pl.pallas_call(
            functools.partial(_gdn_scan_kernel, chunks=sc, reverse=reverse),
            grid=(n_v // SCAN_HEADS, ng),
            in_specs=[per_row(dk), per_row(2 * dv), per_row(2 * dv),
                      pl.BlockSpec((None, SCAN_HEADS, sc, 1, dv), lambda g, b: (d, g, blk(b), 0, 0))]
                     + [tile if spec is None else spec(blk) for spec in extra_specs],
            out_specs=tile,
            out_shape=jax.ShapeDtypeStruct((t, v_dim), BF16),
            scratch_shapes=[pltpu.VMEM((SCAN_HEADS, dk, dv), F32)],
            compiler_params=_cparams(("arbitrary", "arbitrary")),
            name="gdn_scan_bwd" if reverse else "gdn_scan_fwd",
        )(kt, uw, qo, gl_scan, *extra_args)

    o_f = scan(False, (), ())
    z_spec = lambda blk: pl.BlockSpec((rs, wide), lambda g, b: (blk(b), z_off + g))
    nw_spec = lambda blk: pl.BlockSpec((1, dv), lambda g, b: (0, 0))
    return scan(True, (o_f, proj, norm_w.reshape(1, dv).astype(F32)), (None, z_spec, nw_spec))


def kernel(x, c, ada_w, ada_b, norm_w, gdn_in_w, gdn_conv_w, gdn_A_log, gdn_dt_bias, gdn_norm_w, gdn_out_w, cf_pw1_w, cf_pw1_b, cf_dw_w, cf_dw_b, cf_ln_w, cf_ln_b, cf_pw2_w, cf_pw2_b, ffn_in_w, ffn_out_w):
    bsz, t, d = x.shape
    assert bsz == 1
    depth = ada_w.shape[0]
    n_v = gdn_A_log.shape[-1]
    n_qk = n_v // 2
    qk_dim, v_dim = n_qk * HEAD_DIM, n_v * HEAD_DIM
    conv_dim = 2 * qk_dim + v_dim
    row = lambda a: a.reshape(1, -1).astype(F32)
    zero_row = jnp.zeros((1, d), F32)

    mod = ada_modulation(c, ada_w, ada_b)[:, 0].reshape(depth, 6, d)
    xs = x[0]
    h = prenorm(xs, row(norm_w[0, 0]), row(mod[0, 1]), row(mod[0, 0]))
    for i in range(depth):
        j = i // N_MIXERS
        if i % N_MIXERS == 0:
            proj = matmul(h, gdn_in_w, j, 0, conv_dim + v_dim, BF16)
            ba = matmul(h, gdn_in_w, j, conv_dim + v_dim, 4 * n_v, F32)
            gates = gdn_gates(ba, gdn_A_log[j], gdn_dt_bias[j])
            qkv = qkv_conv(proj, gdn_conv_w[j].astype(F32), qk_dim)
            mix = gdn_core(qkv, proj, gates, gdn_norm_w[j], n_qk, n_v)
            w_out, b_out = gdn_out_w[j].astype(BF16), zero_row
        else:
            u = matmul_glu(h, cf_pw1_w, j, row(cf_pw1_b[j]), "glu")
            mix = conformer_conv_ln(u, cf_dw_w[j].astype(F32), row(cf_dw_b[j]), row(cf_ln_w[j]), row(cf_ln_b[j]))
            w_out, b_out = cf_pw2_w[j].astype(BF16), row(cf_pw2_b[j])
        xs, h = matmul_residual(mix, w_out, b_out, xs, row(norm_w[i, 1]), row(mod[i, 2]),
                                row(norm_w[i, 2]), row(mod[i, 4]), row(mod[i, 3]), True)
        f = matmul_glu(h, ffn_in_w, i, jnp.zeros((1, ffn_in_w.shape[2]), F32), "swiglu")
        last = i == depth - 1
        nxt = (zero_row, zero_row, zero_row) if last else (
            row(norm_w[i + 1, 0]), row(mod[i + 1, 1]), row(mod[i + 1, 0]))
        xs, h = matmul_residual(f, ffn_out_w[i].astype(BF16), zero_row, xs, row(norm_w[i, 3]),
                                row(mod[i, 5]), *nxt, not last)
    return xs[None]
```

```python
import functools

import jax
import jax.numpy as jnp
from jax import lax
from jax.experimental import pallas as pl
from jax.experimental.pallas import tpu as pltpu

F32 = jnp.float32
BF16 = jnp.bfloat16

HEAD_DIM = 128
N_MIXERS = 2
CHUNK = 64
RMS_EPS = 1e-6
LN_EPS = 1e-5
L2_EPS = 1e-6

V7X_VMEM_BYTES = 64 * 1024 * 1024
VMEM_LIMIT = V7X_VMEM_BYTES - 12 * 1024 * 1024
HALO = 16


def _cparams(sem):
    return pltpu.CompilerParams(dimension_semantics=sem, vmem_limit_bytes=VMEM_LIMIT)


def _pick(n, prefs):
    for p in prefs:
        if n % p == 0:
            return p
    return n


def _sigmoid(x):
    return 1.0 / (1.0 + jnp.exp(-x))


def _silu(x):
    return x * _sigmoid(x)


def _rms(y):
    return y * lax.rsqrt(jnp.mean(y * y, axis=-1, keepdims=True) + RMS_EPS)


def _ada_kernel(c_ref, w_ref, b_ref, o_ref):
    cond = _silu(c_ref[...])
    o_ref[...] = jnp.dot(cond.astype(BF16), w_ref[...].astype(BF16),
                         preferred_element_type=F32) + b_ref[...]


def ada_modulation(c, ada_w, ada_b):
    depth, d, n = ada_w.shape
    b = c.shape[0]
    assert b <= 8
    cp = jnp.zeros((8, d), F32).at[:b].set(c)
    tn = _pick(n, (1024, 512, 256, 128))
    out = pl.pallas_call(
        _ada_kernel,
        grid=(depth, n // tn),
        in_specs=[pl.BlockSpec((8, d), lambda l, j: (0, 0)),
                  pl.BlockSpec((None, d, tn), lambda l, j: (l, 0, j)),
                  pl.BlockSpec((None, 1, tn), lambda l, j: (l, 0, j))],
        out_specs=pl.BlockSpec((None, 8, tn), lambda l, j: (l, 0, j)),
        out_shape=jax.ShapeDtypeStruct((depth, 8, n), F32),
        compiler_params=_cparams(("arbitrary", "arbitrary")),
        name="ada_modulation",
    )(cp, ada_w, ada_b.reshape(depth, 1, n))
    return out[:, :b]


def _prenorm_kernel(x_ref, nw_ref, sc_ref, sh_ref, h_ref):
    y = _rms(x_ref[...])
    h_ref[...] = (y * nw_ref[...] * (1.0 + sc_ref[...]) + sh_ref[...]).astype(h_ref.dtype)


def prenorm(x, nw, sc, sh):
    t, d = x.shape
    tm = _pick(t, (512, 256, 128, 64, 8))
    row = pl.BlockSpec((1, d), lambda i: (0, 0))
    return pl.pallas_call(
        _prenorm_kernel,
        grid=(t // tm,),
        in_specs=[pl.BlockSpec((tm, d), lambda i: (i, 0)), row, row, row],
        out_specs=pl.BlockSpec((tm, d), lambda i: (i, 0)),
        out_shape=jax.ShapeDtypeStruct((t, d), BF16),
        compiler_params=_cparams(("arbitrary",)),
        name="prenorm",
    )(x, nw, sc, sh)


def _mm_kernel(x_ref, w_ref, o_ref, wb_ref):
    @pl.when(pl.program_id(1) == 0)
    def _():
        wb_ref[...] = w_ref[...].astype(BF16)

    o_ref[...] = jnp.dot(x_ref[...], wb_ref[...], preferred_element_type=F32).astype(o_ref.dtype)


def matmul(x, w, layer, col0, n, out_dtype, tm_prefs=(1024, 512, 256, 128, 64), tn_prefs=(1024, 512, 256, 128)):
    m, k = x.shape
    tm, tn = _pick(m, tm_prefs), _pick(n, tn_prefs)
    assert col0 % tn == 0
    j0 = col0 // tn
    return pl.pallas_call(
        _mm_kernel,
        grid=(n // tn, m // tm),
        in_specs=[pl.BlockSpec((tm, k), lambda j, i: (i, 0)),
                  pl.BlockSpec((None, k, tn), lambda j, i: (layer, 0, j0 + j))],
        out_specs=pl.BlockSpec((tm, tn), lambda j, i: (i, j)),
        out_shape=jax.ShapeDtypeStruct((m, n), out_dtype),
        scratch_shapes=[pltpu.VMEM((k, tn), BF16)],
        compiler_params=_cparams(("arbitrary", "arbitrary")),
        name="matmul",
    )(x, w)


def _mm_glu_kernel(x_ref, wa_ref, wb_ref, ba_ref, bb_ref, o_ref, wab_ref, wbb_ref, *, act):
    @pl.when(pl.program_id(1) == 0)
    def _():
        wab_ref[...] = wa_ref[...].astype(BF16)
        wbb_ref[...] = wb_ref[...].astype(BF16)

    x = x_ref[...]
    a = jnp.dot(x, wab_ref[...], preferred_element_type=F32) + ba_ref[...]
    b = jnp.dot(x, wbb_ref[...], preferred_element_type=F32) + bb_ref[...]
    if act == "swiglu":
        r = _silu(a) * b
    else:
        r = a * _sigmoid(b)
    o_ref[...] = r.astype(o_ref.dtype)


def matmul_glu(x, w, layer, bias, act, tm_prefs=(1024, 512, 256, 128, 64), tn_prefs=(512, 256, 128)):
    m, k = x.shape
    n = w.shape[2] // 2
    tm, tn = _pick(m, tm_prefs), _pick(n, tn_prefs)
    nb = n // tn
    return pl.pallas_call(
        functools.partial(_mm_glu_kernel, act=act),
        grid=(nb, m // tm),
        in_specs=[pl.BlockSpec((tm, k), lambda j, i: (i, 0)),
                  pl.BlockSpec((None, k, tn), lambda j, i: (layer, 0, j)),
                  pl.BlockSpec((None, k, tn), lambda j, i: (layer, 0, j + nb)),
                  pl.BlockSpec((1, tn), lambda j, i: (0, j)),
                  pl.BlockSpec((1, tn), lambda j, i: (0, j + nb))],
        out_specs=pl.BlockSpec((tm, tn), lambda j, i: (i, j)),
        out_shape=jax.ShapeDtypeStruct((m, n), BF16),
        scratch_shapes=[pltpu.VMEM((k, tn), BF16), pltpu.VMEM((k, tn), BF16)],
        compiler_params=_cparams(("arbitrary", "arbitrary")),
        name="matmul_" + act,
    )(x, w, w, bias, bias)


EPI_ROWS = 32


def _mm_res_kernel(a_ref, w_ref, b_ref, x_ref, nwp_ref, g_ref, nwn_ref, sc_ref, sh_ref,
                   xo_ref, *rest, nn):
    *maybe_ho_ref, y_even, y_odd = rest
    i, j = pl.program_id(0), pl.program_id(1)
    n_tiles = pl.num_programs(0) - 1
    tm, tn = y_even.shape[1], y_even.shape[2]
    d = nn * tn
    rows_per_step = tm // nn
    col = lambda ref, jj: ref[:, jj * tn:(jj + 1) * tn]

    @pl.when(jnp.logical_and(i == 0, j == 0))
    def _():
        y_odd[...] = jnp.zeros_like(y_odd)

    def epilogue_slice(y_prev):
        for r in range(rows_per_step // EPI_ROWS):
            rows = pl.ds(r * EPI_ROWS, EPI_ROWS)
            yrows = pl.ds(pl.multiple_of(j * rows_per_step + r * EPI_ROWS, EPI_ROWS), EPI_ROWS)
            ys = [y_prev[jj, yrows, :] + col(b_ref, jj) for jj in range(nn)]
            inv = lax.rsqrt(sum(jnp.sum(y * y, axis=-1, keepdims=True) for y in ys) / d + RMS_EPS)
            xns = [x_ref[rows, jj * tn:(jj + 1) * tn] + (1.0 + col(g_ref, jj)) * (ys[jj] * inv * col(nwp_ref, jj))
                   for jj in range(nn)]
            for jj in range(nn):
                xo_ref[rows, jj * tn:(jj + 1) * tn] = xns[jj]
            if maybe_ho_ref:
                inv2 = lax.rsqrt(sum(jnp.sum(v * v, axis=-1, keepdims=True) for v in xns) / d + RMS_EPS)
                for jj in range(nn):
                    hn = xns[jj] * inv2 * col(nwn_ref, jj) * (1.0 + col(sc_ref, jj)) + col(sh_ref, jj)
                    maybe_ho_ref[0][rows, jj * tn:(jj + 1) * tn] = hn.astype(BF16)

    def step(y_cur, y_prev):
        y_cur[j] = jnp.dot(a_ref[...], w_ref[...], preferred_element_type=F32)
        epilogue_slice(y_prev)

    even, busy = lax.rem(i, 2) == 0, i < n_tiles
    pl.when(jnp.logical_and(busy, even))(lambda: step(y_even, y_odd))
    pl.when(jnp.logical_and(busy, jnp.logical_not(even)))(lambda: step(y_odd, y_even))
    pl.when(jnp.logical_and(jnp.logical_not(busy), even))(lambda: epilogue_slice(y_odd))
    pl.when(jnp.logical_and(jnp.logical_not(busy), jnp.logical_not(even)))(lambda: epilogue_slice(y_even))


def matmul_residual(a, w, bias, x, nw_post, gate, nw_next, sc_next, sh_next, emit_next,
                    tm_prefs=(512, 256, 128, 64)):
    m, kdim = a.shape
    d = w.shape[1]
    tm = _pick(m, tm_prefs)
    tn = _pick(d, (1024, 512, 256, 128) if kdim <= 4096 else (512, 256, 128))
    nn = d // tn
    n_tiles = m // tm
    assert (tm // nn) % EPI_ROWS == 0
    row = pl.BlockSpec((1, d), lambda i, j: (0, 0))
    tile = pl.BlockSpec((tm // nn, d), lambda i, j: (jnp.maximum((i - 1) * nn + j, 0), 0))
    out_shape = [jax.ShapeDtypeStruct((m, d), F32)]
    out_specs = [tile]
    if emit_next:
        out_shape.append(jax.ShapeDtypeStruct((m, d), BF16))
        out_specs.append(tile)
    res = pl.pallas_call(
        functools.partial(_mm_res_kernel, nn=nn),
        grid=(n_tiles + 1, nn),
        in_specs=[pl.BlockSpec((tm, kdim), lambda i, j: (jnp.minimum(i, n_tiles - 1), 0)),
                  pl.BlockSpec((kdim, tn), lambda i, j: (0, j)),
                  row, tile, row, row, row, row, row],
        out_specs=out_specs,
        out_shape=out_shape,
        scratch_shapes=[pltpu.VMEM((nn, tm, tn), F32), pltpu.VMEM((nn, tm, tn), F32)],
        compiler_params=_cparams(("arbitrary", "arbitrary")),
        name="matmul_residual",
    )(a, w, bias, x, nw_post, gate, nw_next, sc_next, sh_next)
    return (res[0], res[1]) if emit_next else (res[0], None)


LANES = 128


def _halo_specs(tm, tc, t):
    per = tm // HALO
    last = t // HALO - 1
    return [pl.BlockSpec((HALO, tc), lambda i, j: (jnp.maximum(i * per - 1, 0), j)),
            pl.BlockSpec((tm, tc), lambda i, j: (i, j)),
            pl.BlockSpec((HALO, tc), lambda i, j: (jnp.minimum((i + 1) * per, last), j))]


def _fill_ext(up_ref, uc_ref, un_ref, ext_ref):
    i, n = pl.program_id(0), pl.num_programs(0)
    tm = uc_ref.shape[0]
    for s in range(ext_ref.shape[0]):
        cols = slice(s * LANES, (s + 1) * LANES)
        ext_ref[s, 0:HALO, :] = jnp.where(i > 0, up_ref[:, cols].astype(F32), 0.0)
        ext_ref[s, HALO:HALO + tm, :] = uc_ref[:, cols].astype(F32)
        ext_ref[s, HALO + tm:HALO + tm + HALO, :] = jnp.where(i < n - 1, un_ref[:, cols].astype(F32), 0.0)


def _conv_rows(ext_ref, w_ref, slab, base, rows, width):
    pad = width // 2
    cols = slice(slab * LANES, (slab + 1) * LANES)
    acc = None
    for j in range(width):
        term = ext_ref[slab, pl.ds(base + (HALO - pad + j), rows), :] * w_ref[j:j + 1, cols]
        acc = term if acc is None else acc + term
    return acc


CONF_ROWS = 64
LN_ROWS = 16


def _conf_conv_kernel(up_ref, uc_ref, un_ref, w_ref, b_ref, lnw_ref, lnb_ref, o_ref, ext_ref, cv_ref, *, width):
    _fill_ext(up_ref, uc_ref, un_ref, ext_ref)
    n_slabs = ext_ref.shape[0]
    d = n_slabs * LANES

    def step(r, carry):
        base = pl.multiple_of(r * CONF_ROWS, CONF_ROWS)
        for s in range(n_slabs):
            cols = slice(s * LANES, (s + 1) * LANES)
            cv_ref[:, cols] = _conv_rows(ext_ref, w_ref, s, base, CONF_ROWS, width) + b_ref[:, cols]
        for r0 in range(0, CONF_ROWS, LN_ROWS):
            u = cv_ref[r0:r0 + LN_ROWS, :]
            xc = u - jnp.mean(u, axis=-1, keepdims=True)
            y = xc * lax.rsqrt(jnp.mean(xc * xc, axis=-1, keepdims=True) + LN_EPS)
            y = y * lnw_ref[...] + lnb_ref[...]
            o_ref[pl.ds(base + r0, LN_ROWS), :] = _silu(y).astype(o_ref.dtype)
        return carry

    lax.fori_loop(0, uc_ref.shape[0] // CONF_ROWS, step, 0)


def conformer_conv_ln(u, dw_w, dw_b, ln_w, ln_b):
    t, d = u.shape
    width = dw_w.shape[0]
    assert width // 2 <= HALO and d % LANES == 0
    tm = _pick(t, (256, 128, 64))
    row = pl.BlockSpec((1, d), lambda i, j: (0, 0))
    return pl.pallas_call(
        functools.partial(_conf_conv_kernel, width=width),
        grid=(t // tm, 1),
        in_specs=_halo_specs(tm, d, t) + [pl.BlockSpec((width, d), lambda i, j: (0, 0)), row, row, row],
        out_specs=pl.BlockSpec((tm, d), lambda i, j: (i, 0)),
        out_shape=jax.ShapeDtypeStruct((t, d), BF16),
        scratch_shapes=[pltpu.VMEM((d // LANES, tm + 2 * HALO, LANES), F32), pltpu.VMEM((CONF_ROWS, d), F32)],
        compiler_params=_cparams(("arbitrary", "arbitrary")),
        name="conformer_conv_ln",
    )(u, u, u, dw_w, dw_b, ln_w, ln_b)


QKV_ROWS = 64


def _qkv_conv_kernel(up_ref, uc_ref, un_ref, w_ref, o_ref, ext_ref, *, width, q_tiles, qk_tiles, q_scale):
    _fill_ext(up_ref, uc_ref, un_ref, ext_ref)
    j = pl.program_id(1)
    scale = jnp.where(j < q_tiles, q_scale, 1.0).astype(F32)

    def make_step(normalise):
        def step(r, carry):
            base = pl.multiple_of(r * QKV_ROWS, QKV_ROWS)
            for s in range(ext_ref.shape[0]):
                y = _silu(_conv_rows(ext_ref, w_ref, s, base, QKV_ROWS, width))
                if normalise:
                    y = y * (lax.rsqrt(jnp.sum(y * y, axis=-1, keepdims=True) + L2_EPS) * scale)
                o_ref[pl.ds(base, QKV_ROWS), s * LANES:(s + 1) * LANES] = y.astype(o_ref.dtype)
            return carry
        return step

    nsteps = uc_ref.shape[0] // QKV_ROWS

    @pl.when(j < qk_tiles)
    def _():
        lax.fori_loop(0, nsteps, make_step(True), 0)

    @pl.when(j >= qk_tiles)
    def _():
        lax.fori_loop(0, nsteps, make_step(False), 0)


def qkv_conv(proj, conv_w, qk_dim):
    t = proj.shape[0]
    width, cdim = conv_w.shape
    assert HEAD_DIM == LANES and width // 2 <= HALO
    tm = _pick(t, (512, 256, 128, 64, 32))
    tc = _pick(qk_dim, (2048, 1024, 512, 256, 128))
    return pl.pallas_call(
        functools.partial(_qkv_conv_kernel, width=width, q_tiles=qk_dim // tc,
                          qk_tiles=2 * qk_dim // tc, q_scale=HEAD_DIM ** -0.5),
        grid=(t // tm, cdim // tc),
        in_specs=_halo_specs(tm, tc, t) + [pl.BlockSpec((width, tc), lambda i, j: (0, j))],
        out_specs=pl.BlockSpec((tm, tc), lambda i, j: (i, j)),
        out_shape=jax.ShapeDtypeStruct((t, cdim), BF16),
        scratch_shapes=[pltpu.VMEM((tc // LANES, tm + 2 * HALO, LANES), F32)],
        compiler_params=_cparams(("arbitrary", "arbitrary")),
        name="qkv_conv",
    )(proj, proj, proj, conv_w)


def _split3(x):
    hi = x.astype(BF16)
    r = x - hi.astype(F32)
    mid = r.astype(BF16)
    lo = (r - mid.astype(F32)).astype(BF16)
    return hi, mid, lo


def _gate_kernel(ba_ref, alog_ref, dt_ref, o_ref, *, n_heads):
    tm = ba_ref.shape[0]
    ri = lax.broadcasted_iota(jnp.int32, (CHUNK, CHUNK), 0)
    ci = lax.broadcasted_iota(jnp.int32, (CHUNK, CHUNK), 1)
    tril = jnp.where(ri >= ci, 1.0, 0.0).astype(BF16)
    triu = jnp.where(ri <= ci, 1.0, 0.0).astype(BF16)
    lane = lax.broadcasted_iota(jnp.int32, (CHUNK, 4 * n_heads), 1)
    for c in range(tm // CHUNK):
        ba = ba_ref[c * CHUNK:(c + 1) * CHUNK, :]
        beta = _sigmoid(ba)
        x = ba + dt_ref[...]
        softplus = jnp.maximum(x, 0.0) + jnp.log(1.0 + jnp.exp(-jnp.abs(x)))
        g = -jnp.exp(alog_ref[...]) * softplus
        pre = jnp.zeros_like(g)
        suf = jnp.zeros_like(g)
        for part in _split3(g):
            pre = pre + jnp.dot(tril, part, preferred_element_type=F32)
            suf = suf + jnp.dot(triu, part, preferred_element_type=F32)
        cum = jnp.where(lane < 3 * n_heads, pre, suf)
        o_ref[c * CHUNK:(c + 1) * CHUNK, :] = jnp.where(lane < 2 * n_heads, beta, cum)


def _proj_gate_kernel(h_ref, w_ref, alog_ref, dt_ref, o_ref, ba_ref, *, n_heads):
    ba_ref[...] = jnp.dot(h_ref[...], w_ref[...].astype(BF16), preferred_element_type=F32)
    _gate_kernel(ba_ref, alog_ref, dt_ref, o_ref, n_heads=n_heads)


def gdn_proj_gates(h, w, layer, col0, a_log, dt_bias):
    t, k = h.shape
    nv = a_log.shape[-1]
    n = 4 * nv
    assert col0 % n == 0
    tm = _pick(t, (512, 256, 128, 64))
    zeros = jnp.zeros((2 * nv,), F32)
    alog_row = jnp.concatenate([zeros, a_log.reshape(-1).astype(F32)]).reshape(1, n)
    dt_row = jnp.concatenate([zeros, dt_bias.reshape(-1).astype(F32)]).reshape(1, n)
    row = pl.BlockSpec((1, n), lambda i: (0, 0))
    return pl.pallas_call(
        functools.partial(_proj_gate_kernel, n_heads=nv),
        grid=(t // tm,),
        in_specs=[pl.BlockSpec((tm, k), lambda i: (i, 0)),
                  pl.BlockSpec((None, k, n), lambda i: (layer, 0, col0 // n)), row, row],
        out_specs=pl.BlockSpec((tm, n), lambda i: (i, 0)),
        out_shape=jax.ShapeDtypeStruct((t, n), F32),
        scratch_shapes=[pltpu.VMEM((tm, n), F32)],
        compiler_params=_cparams(("arbitrary",)),
        name="gdn_proj_gates",
    )(h, w, alog_row, dt_row)


def _bdot(a, b, ca=2, cb=1):
    return lax.dot_general(a.astype(BF16), b.astype(BF16), (((ca,), (cb,)), ((0,), (0,))),
                           preferred_element_type=F32)


N_STREAMS = 4


class _PackedMasks:
    def __init__(self):
        c, w = CHUNK, N_STREAMS * CHUNK
        ri = lax.broadcasted_iota(jnp.int32, (c, w), 0)
        li = lax.broadcasted_iota(jnp.int32, (c, w), 1)
        ci = jnp.bitwise_and(li, c - 1)
        fwd = li < (N_STREAMS // 2) * c
        self.eye = ri == ci
        bwd = jnp.logical_not(fwd)
        self.incl = jnp.logical_or(jnp.logical_and(fwd, ri >= ci), jnp.logical_and(bwd, ri <= ci))
        self.strict = jnp.logical_or(jnp.logical_and(fwd, ri > ci), jnp.logical_and(bwd, ri < ci))
        self.seg = [jnp.right_shift(li, 6) == s for s in range(N_STREAMS)]
        self.same16 = jnp.right_shift(ri, 4) == jnp.right_shift(ci, 4)
        self.same32 = jnp.right_shift(ri, 5) == jnp.right_shift(ci, 5)
        rb = lax.broadcasted_iota(jnp.int32, (w, w), 0)
        lb = lax.broadcasted_iota(jnp.int32, (w, w), 1)
        self.blockdiag = jnp.right_shift(rb, 6) == jnp.right_shift(lb, 6)


    def spread(self, cols):
        out = cols[-1]
        for s in range(N_STREAMS - 2, -1, -1):
            out = jnp.where(self.seg[s], cols[s], out)
        return out

    def columns(self, row):
        z = jnp.where(self.eye, row, 0.0)
        return [jnp.sum(jnp.where(self.seg[s], z, 0.0), axis=-1, keepdims=True) for s in range(N_STREAMS)]

    def bd(self, xp):
        return jnp.where(self.blockdiag, jnp.concatenate([xp] * N_STREAMS, axis=-2), 0.0).astype(BF16)


def _inv_unit_triangular_packed(lp, m):
    mm = lambda xp, yp: _bdot(xp, m.bd(yp))
    l16 = jnp.where(m.same16, lp, 0.0)
    c32 = jnp.where(jnp.logical_and(m.same32, jnp.logical_not(m.same16)), lp, 0.0)
    c64 = jnp.where(m.same32, 0.0, lp)
    p = jnp.where(m.eye, 1.0, 0.0) - l16
    sq = mm(l16, l16)
    for _ in range(2):
        both = mm(jnp.concatenate([p, sq], axis=-2), sq)
        p, sq = p + both[:, :CHUNK], both[:, CHUNK:]
    p = p + mm(p, sq)
    y = p - mm(mm(p, c32), p)
    return y - mm(mm(y, c64), y)


def _gdn_prep_kernel(q_ref, k_ref, v_ref, tab_ref, gl_ref, kt_ref, uw_ref, qo_ref, *, chunks):
    c, dk, g = CHUNK, HEAD_DIM, PREP_GROUP
    m = _PackedMasks()

    def group_step(i, carry):
        j0 = pl.multiple_of(i * g, g)
        rows = pl.ds(pl.multiple_of(i * (g * c), g * c), g * c)
        q = q_ref[rows, :].reshape(g, c, dk)
        k = k_ref[rows, :].reshape(g, c, dk)
        v = v_ref[rows, :].reshape(g, c, 2 * dk)
        tab, gl = tab_ref[pl.ds(j0, g)], gl_ref[pl.ds(j0, g)]
        beta_p, g_p = tab[:, 0:1, :], tab[:, 1:2, :]
        x = _bdot(jnp.concatenate([q, k], axis=1), jnp.concatenate([k, k], axis=1), 2, 2)
        qk_p = jnp.concatenate([x[:, :c], x[:, :c]], axis=2)
        kk_p = jnp.concatenate([x[:, c:], x[:, c:]], axis=2)
        g_cols = m.columns(g_p)
        g_cb = m.spread(g_cols)
        beta_cb = m.spread(m.columns(beta_p))
        decay = jnp.exp(jnp.where(m.incl, g_cb - g_p, -jnp.inf))
        l_p = jnp.where(m.strict, kk_p * decay * beta_cb, 0.0)
        a_p = qk_p * decay
        t_p = _inv_unit_triangular_packed(l_p, m)
        k_st = jnp.concatenate([k] * N_STREAMS, axis=1)
        v_st = jnp.concatenate([v[:, :, :dk], v[:, :, dk:]] * (N_STREAMS // 2), axis=1)
        g_cst = jnp.concatenate(g_cols, axis=1)
        eg_cst = jnp.exp(g_cst)
        vk = jnp.concatenate([v_st, (k_st.astype(F32) * eg_cst).astype(BF16)], axis=2)
        uw = _bdot(m.bd(t_p * beta_p), vk).astype(BF16)
        a_uw = _bdot(m.bd(a_p), uw)
        q_st = jnp.concatenate([q] * N_STREAMS, axis=1).astype(F32)
        q_prime = q_st * eg_cst - a_uw[:, :, dk:]
        qo = jnp.concatenate([q_prime, a_uw[:, :, :dk]], axis=2).astype(qo_ref.dtype)
        gl_st = jnp.concatenate([jnp.broadcast_to(gl[:, s:s + 1, :], (g, c, dk)) for s in range(N_STREAMS)], axis=1)
        kt_st = (k_st.astype(F32) * jnp.exp(gl_st - g_cst)).astype(BF16)
        for s in range(N_STREAMS):
            seg = slice(s * c, (s + 1) * c)
            d, vh = divmod(s, 2)
            kt_ref[d, vh, rows, :] = kt_st[:, seg].reshape(g * c, dk)
            uw_ref[d, vh, rows, :] = uw[:, seg].reshape(g * c, 2 * dk)
            qo_ref[d, vh, rows, :] = qo[:, seg].reshape(g * c, 2 * dk)
        return carry

    lax.fori_loop(0, chunks // g, group_step, 0)


SCAN_HEADS = 16


def _gdn_scan_kernel(kt_ref, uw_ref, qo_ref, gl_ref, *rest, chunks, reverse):
    c, dk = CHUNK, HEAD_DIM
    if reverse:
        of_ref, z_ref, nw_ref, out_ref, s_ref = rest
    else:
        out_ref, s_ref = rest

    @pl.when(pl.program_id(1) == 0)
    def _():
        s_ref[...] = jnp.zeros_like(s_ref)

    def chunk_step(j, carry):
        cj = chunks - 1 - j if reverse else j
        rows = pl.ds(pl.multiple_of(cj * c, c), c)
        pr_all = [lax.dot_general(kt_ref[vh, rows, :], uw_ref[vh, rows, :], (((0,), (0,)), ((), ())),
                                  preferred_element_type=F32) for vh in range(SCAN_HEADS)]
        for pair in range(SCAN_HEADS // 2):
            vhs = (2 * pair, 2 * pair + 1)
            prs = [pr_all[vh] for vh in vhs]
            qos = [qo_ref[vh, rows, :] for vh in vhs]
            ss = [s_ref[vh] for vh in vhs]
            lhs = jnp.concatenate([jnp.concatenate([pr[:, dk:].astype(BF16), qo[:, :dk]], axis=0)
                                   for pr, qo in zip(prs, qos)], axis=1)
            zero = jnp.zeros((dk, dk), BF16)
            s_bd = jnp.concatenate([jnp.concatenate([ss[0].astype(BF16), zero], axis=1),
                                    jnp.concatenate([zero, ss[1].astype(BF16)], axis=1)], axis=0)
            x = jnp.dot(lhs, s_bd, preferred_element_type=F32)
            for n, vh in enumerate(vhs):
                xs = x[:, n * dk:(n + 1) * dk]
                cols = slice(vh * dk, (vh + 1) * dk)
                o = xs[dk:] + qos[n][:, dk:].astype(F32)
                if reverse:
                    tot = o + of_ref[rows, cols].astype(F32)
                    o = _rms(tot) * nw_ref[...] * _silu(z_ref[rows, cols].astype(F32))
                out_ref[rows, cols] = o.astype(out_ref.dtype)
                s_ref[vh] = ss[n] * jnp.exp(gl_ref[vh, cj]) + prs[n][:, :dk] - xs[:dk]
        return carry

    lax.fori_loop(0, chunks, chunk_step, 0)


PREP_GROUP = 8
PREP_CHUNKS = 32
SCAN_CHUNKS = 8


def gdn_core(qkv, proj, gates, norm_w, n_qk, n_v):
    t = qkv.shape[0]
    c, dk = CHUNK, HEAD_DIM
    dv = dk
    assert n_v == 2 * n_qk and t % c == 0 and n_v % SCAN_HEADS == 0
    nc = t // c
    qk_dim, v_dim = n_qk * dk, n_v * dv
    beta = gates[:, :2 * n_v].reshape(nc, c, 2, n_qk, 2)
    cum = gates[:, 2 * n_v:].reshape(nc, c, 2, n_qk, 2)
    packed = lambda a: a.transpose(3, 0, 2, 4, 1).reshape(n_qk, nc, 1, N_STREAMS * c)
    tab = jnp.concatenate([packed(beta), packed(cum), jnp.zeros((n_qk, nc, 6, N_STREAMS * c), F32)], axis=2)
    last = jnp.stack([cum[:, c - 1, 0], cum[:, 0, 1]])
    gl_prep = last.transpose(2, 1, 0, 3).reshape(n_qk, nc, N_STREAMS, 1)
    gl_prep = jnp.broadcast_to(jnp.pad(gl_prep, ((0, 0), (0, 0), (0, 8 - N_STREAMS), (0, 0))), (n_qk, nc, 8, dk))
    gl_scan = jnp.broadcast_to(last.reshape(2, nc, n_v).transpose(0, 2, 1)[..., None, None], (2, n_v, nc, 1, dv))

    pc = _pick(nc, (PREP_CHUNKS, PREP_GROUP))
    assert nc % pc == 0 and pc % PREP_GROUP == 0
    rb = pc * c
    per_row = lambda width: pl.BlockSpec((2, 2, rb, width), lambda h, b: (0, h, b, 0))
    kt, uw, qo = pl.pallas_call(
        functools.partial(_gdn_prep_kernel, chunks=pc),
        grid=(n_qk, nc // pc),
        in_specs=[pl.BlockSpec((rb, dk), lambda h, b: (b, h)),
                  pl.BlockSpec((rb, dk), lambda h, b: (b, n_qk + h)),
                  pl.BlockSpec((rb, 2 * dv), lambda h, b: (b, qk_dim // dv + h)),
                  pl.BlockSpec((None, pc, 8, N_STREAMS * c), lambda h, b: (h, b, 0, 0)),
                  pl.BlockSpec((None, pc, 8, dk), lambda h, b: (h, b, 0, 0))],
        out_specs=[per_row(dk), per_row(2 * dv), per_row(2 * dv)],
        out_shape=[jax.ShapeDtypeStruct((2, n_v, t, dk), BF16),
                   jax.ShapeDtypeStruct((2, n_v, t, 2 * dv), BF16),
                   jax.ShapeDtypeStruct((2, n_v, t, 2 * dv), BF16)],
        compiler_params=_cparams(("arbitrary", "arbitrary")),
        name="gdn_prep",
    )(qkv, qkv, qkv, tab, gl_prep)

    sc = _pick(nc, (SCAN_CHUNKS, 4, 2, 1))
    ng = nc // sc
    rs = sc * c
    wide = SCAN_HEADS * dv
    z_off = (2 * qk_dim + v_dim) // wide

    def scan(reverse, extra_args, extra_specs):
        d = int(reverse)
        blk = (lambda b: ng - 1 - b) if reverse else (lambda b: b)
        tile = pl.BlockSpec((rs, wide), lambda g, b: (blk(b), g))
        per_row = lambda width: pl.BlockSpec((None, SCAN_HEADS, rs, width), lambda g, b: (d, g, blk(b), 0))
        return pl.pallas_call(
            functools.partial(_gdn_scan_kernel, chunks=sc, reverse=reverse),
            grid=(n_v // SCAN_HEADS, ng),
            in_specs=[per_row(dk), per_row(2 * dv), per_row(2 * dv),
                      pl.BlockSpec((None, SCAN_HEADS, sc, 1, dv), lambda g, b: (d, g, blk(b), 0, 0))]
                     + [tile if spec is None else spec(blk) for spec in extra_specs],
            out_specs=tile,
            out_shape=jax.ShapeDtypeStruct((t, v_dim), BF16),
            scratch_shapes=[pltpu.VMEM((SCAN_HEADS, dk, dv), F32)],
            compiler_params=_cparams(("arbitrary", "arbitrary")),
            name="gdn_scan_bwd" if reverse else "gdn_scan_fwd",
        )(kt, uw, qo, gl_scan, *extra_args)

    o_f = scan(False, (), ())
    z_spec = lambda blk: pl.BlockSpec((rs, wide), lambda g, b: (blk(b), z_off + g))
    nw_spec = lambda blk: pl.BlockSpec((1, dv), lambda g, b: (0, 0))
    return scan(True, (o_f, proj, norm_w.reshape(1, dv).astype(F32)), (None, z_spec, nw_spec))


def kernel(x, c, ada_w, ada_b, norm_w, gdn_in_w, gdn_conv_w, gdn_A_log, gdn_dt_bias, gdn_norm_w, gdn_out_w, cf_pw1_w, cf_pw1_b, cf_dw_w, cf_dw_b, cf_ln_w, cf_ln_b, cf_pw2_w, cf_pw2_b, ffn_in_w, ffn_out_w):
    bsz, t, d = x.shape
    assert bsz == 1
    depth = ada_w.shape[0]
    n_v = gdn_A_log.shape[-1]
    n_qk = n_v // 2
    qk_dim, v_dim = n_qk * HEAD_DIM, n_v * HEAD_DIM
    conv_dim = 2 * qk_dim + v_dim
    row = lambda a: a.reshape(1, -1).astype(F32)
    zero_row = jnp.zeros((1, d), F32)

    mod = ada_modulation(c, ada_w, ada_b)[:, 0].reshape(depth, 6, d)
    xs = x[0]
    h = prenorm(xs, row(norm_w[0, 0]), row(mod[0, 1]), row(mod[0, 0]))
    for i in range(depth):
        j = i // N_MIXERS
        if i % N_MIXERS == 0:
            proj = matmul(h, gdn_in_w, j, 0, conv_dim + v_dim, BF16)
            gates = gdn_proj_gates(h, gdn_in_w, j, conv_dim + v_dim, gdn_A_log[j], gdn_dt_bias[j])
            qkv = qkv_conv(proj, gdn_conv_w[j].astype(F32), qk_dim)
            mix = gdn_core(qkv, proj, gates, gdn_norm_w[j], n_qk, n_v)
            w_out, b_out = gdn_out_w[j].astype(BF16), zero_row
        else:
            u = matmul_glu(h, cf_pw1_w, j, row(cf_pw1_b[j]), "glu")
            mix = conformer_conv_ln(u, cf_dw_w[j].astype(F32), row(cf_dw_b[j]), row(cf_ln_w[j]), row(cf_ln_b[j]))
            w_out, b_out = cf_pw2_w[j].astype(BF16), row(cf_pw2_b[j])
        xs, h = matmul_residual(mix, w_out, b_out, xs, row(norm_w[i, 1]), row(mod[i, 2]),
                                row(norm_w[i, 2]), row(mod[i, 4]), row(mod[i, 3]), True)
        f = matmul_glu(h, ffn_in_w, i, jnp.zeros((1, ffn_in_w.shape[2]), F32), "swiglu")
        last = i == depth - 1
        nxt = (zero_row, zero_row, zero_row) if last else (
            row(norm_w[i + 1, 0]), row(mod[i + 1, 1]), row(mod[i + 1, 0]))
        xs, h = matmul_residual(f, ffn_out_w[i].astype(BF16), zero_row, xs, row(norm_w[i, 3]),
                                row(mod[i, 5]), *nxt, not last)
    return xs[None]
```
